```python
import jax, jax.numpy as jnp
from jax import lax
import numpy as np

D_MODEL = 1024
BATCH = 2
SEQ = 8192
DEPTH = 1
DEC_BATCH = 128
DEC_SEQ = 1
PAST_LEN = 8192
PAGE_SIZE = 128

RW_HEADS = 8
RW_HEAD_DIM = 64
RW_WIDTH = RW_HEADS * RW_HEAD_DIM
DECAY_LORA = 64
AAA_LORA = 64
GATE_LORA = 128
RW_COLS = 3 * RW_WIDTH + DECAY_LORA + AAA_LORA + GATE_LORA
RW_SPLITS = (RW_WIDTH, 2 * RW_WIDTH, 3 * RW_WIDTH, 3 * RW_WIDTH + DECAY_LORA,
             3 * RW_WIDTH + DECAY_LORA + AAA_LORA)
GN_EPS = 64e-5
SWA_HEADS = 8
SWA_KV_HEADS = 2
SWA_GROUPS = SWA_HEADS // SWA_KV_HEADS
SWA_HEAD_DIM = 64
SWA_Q = SWA_HEADS * SWA_HEAD_DIM
SWA_KV = SWA_KV_HEADS * SWA_HEAD_DIM
WINDOW = 128
BLOCK = 128
ROPE_THETA = 500000.0
ROPE_DIM = SWA_HEAD_DIM // 4
MEM_LEN = 256
MEM_HEADS = 4
MEM_HEAD_DIM = 128
MEM_WIDTH = MEM_HEADS * MEM_HEAD_DIM
N_BRANCH = 3
D_FF = 4 * D_MODEL
NORM_EPS = 1e-5
IN_COLS = RW_COLS + SWA_Q + 2 * SWA_KV + MEM_WIDTH + N_BRANCH * D_MODEL
IN_SPLITS = (RW_COLS, RW_COLS + SWA_Q, RW_COLS + SWA_Q + SWA_KV, RW_COLS + SWA_Q + 2 * SWA_KV,
             RW_COLS + SWA_Q + 2 * SWA_KV + MEM_WIDTH)

kernel_name = 'hybrid_rwkv7_swa_sink_memxattn_decode_step'

F32 = jnp.float32


def _rmsnorm(x, g):
    xf = x.astype(F32)
    y = xf * lax.rsqrt(jnp.mean(xf * xf, axis=-1, keepdims=True) + NORM_EPS)
    return (y * g.astype(F32)).astype(x.dtype)


def _rope(x, pos):
    half = ROPE_DIM // 2
    inv_freq = jnp.power(jnp.float32(ROPE_THETA), -jnp.arange(half, dtype=F32) * (2.0 / ROPE_DIM))
    ang = pos.astype(F32)[:, None] * inv_freq[None, :]
    cos = jnp.cos(ang)[:, None, :]
    sin = jnp.sin(ang)[:, None, :]
    xf = x.astype(F32)
    x1 = xf[..., :half]
    x2 = xf[..., half:ROPE_DIM]
    out = jnp.concatenate([x1 * cos - x2 * sin, x2 * cos + x1 * sin, xf[..., ROPE_DIM:]], axis=-1)
    return out.astype(x.dtype)


def _rwkv_scan(S0, r, decay, k, v, kk, a):
    def step(S, inp):
        r_t, w_t, k_t, v_t, kk_t, a_t = inp
        sa = jnp.einsum('bhvk,bhk->bhv', S, -kk_t)
        S = (S * w_t[:, :, None, :] + sa[..., None] * (kk_t * a_t)[:, :, None, :]
             + v_t[..., None] * k_t[:, :, None, :])
        y = jnp.einsum('bhvk,bhk->bhv', S, r_t)
        return S, y
    xs = tuple(jnp.moveaxis(t, 1, 0) for t in (r, decay, k, v, kk, a))
    S, ys = lax.scan(step, S0.astype(F32), xs)
    return S, jnp.moveaxis(ys, 0, 1)


def _rwkv_branch(z, z_prev, S0, p):
    B, T, _ = z.shape
    zs = z + (z_prev - z) * p['rw_mu']
    r, k, v, wd, ad, gd = jnp.split(zs, RW_SPLITS, axis=-1)
    w_log = -jax.nn.softplus(-(p['rw_w0'] + jnp.tanh(wd) @ p['rw_w2'])) - 0.5
    decay = jnp.exp(-jnp.exp(w_log.astype(F32)))
    a = jax.nn.sigmoid(p['rw_a0'] + ad @ p['rw_a2'])
    g = jax.nn.sigmoid(gd) @ p['rw_g2']

    def heads(t):
        return t.astype(F32).reshape(B, T, RW_HEADS, RW_HEAD_DIM)

    kk = heads(k * p['rw_k_k'])
    kk = kk / jnp.maximum(jnp.sqrt(jnp.sum(kk * kk, axis=-1, keepdims=True)), 1e-12)
    kh = heads(k * (1.0 + (a - 1.0) * p['rw_k_a']))
    rh, vh, ah, dh = heads(r), heads(v), heads(a), heads(decay)
    S_new, y = _rwkv_scan(S0, rh, dh, kh, vh, kk, ah)
    mu = jnp.mean(y, axis=-1, keepdims=True)
    var = jnp.mean(jnp.square(y - mu), axis=-1, keepdims=True)
    y = ((y - mu) * lax.rsqrt(var + GN_EPS)).reshape(B, T, RW_WIDTH)
    y = y * p['rw_ln_w'].astype(F32) + p['rw_ln_b'].astype(F32)
    bonus = jnp.sum(rh * kh * p['rw_r_k'].astype(F32), axis=-1, keepdims=True) * vh
    out = (y + bonus.reshape(B, T, RW_WIDTH)) * g.astype(F32)
    return out.astype(z.dtype), S_new


def _sink_attn(q, k, v, q_pos, k_pos, sinks):
    s = jnp.einsum('...qhgd,...shd->...hgqs', q.astype(F32), k.astype(F32)) * (SWA_HEAD_DIM ** -0.5)
    rel = q_pos[..., :, None] - k_pos[..., None, :]
    valid = (rel >= 0) & (rel <= WINDOW) & (k_pos[..., None, :] >= 0)
    s = jnp.where(valid[..., None, None, :, :], s, -jnp.inf)
    sink = jnp.broadcast_to(sinks.astype(F32).reshape(SWA_KV_HEADS, SWA_GROUPS, 1, 1), s.shape[:-1] + (1,))
    pr = jax.nn.softmax(jnp.concatenate([s, sink], axis=-1), axis=-1)[..., :-1]
    o = jnp.einsum('...hgqs,...shd->...qhgd', pr, v.astype(F32))
    return o.astype(v.dtype)


def _swa_prompt(q, k, v, sinks):
    B, T = q.shape[0], q.shape[1]
    nb = T // BLOCK
    qb = q.reshape(B, nb, BLOCK, SWA_KV_HEADS, SWA_GROUPS, SWA_HEAD_DIM)
    kb = k.reshape(B, nb, BLOCK, SWA_KV_HEADS, SWA_HEAD_DIM)
    vb = v.reshape(B, nb, BLOCK, SWA_KV_HEADS, SWA_HEAD_DIM)
    pad = ((0, 0), (1, 0), (0, 0), (0, 0), (0, 0))
    k_band = jnp.concatenate([jnp.pad(kb, pad)[:, :-1], kb], axis=2)
    v_band = jnp.concatenate([jnp.pad(vb, pad)[:, :-1], vb], axis=2)
    pos = jnp.arange(T, dtype=jnp.int32).reshape(nb, BLOCK)
    k_pos = jnp.concatenate([pos - BLOCK, pos], axis=1)
    o = _sink_attn(qb, k_band, v_band, pos, k_pos, sinks).reshape(B, T, SWA_Q)
    keep = min(WINDOW, T)
    return o, k[:, T - keep:], v[:, T - keep:]


def _swa_sample(q, k, v, k_past, v_past, sinks):
    B, T = q.shape[0], q.shape[1]
    W = k_past.shape[1]
    q_pos = PAST_LEN + jnp.arange(T, dtype=jnp.int32)
    k_pos = jnp.concatenate([PAST_LEN - W + jnp.arange(W, dtype=jnp.int32), q_pos])
    k_all = jnp.concatenate([k_past.astype(k.dtype), k], axis=1)
    v_all = jnp.concatenate([v_past.astype(v.dtype), v], axis=1)
    o = _sink_attn(q, k_all, v_all, q_pos, k_pos, sinks).reshape(B, T, SWA_Q)
    return o, k_all[:, -W:], v_all[:, -W:]


def _mem_kv(mem, mem_norm, w_mem_kv, xk_norm):
    B, M, _ = mem.shape
    kv = _rmsnorm(mem, mem_norm) @ w_mem_kv
    mk = kv[..., :MEM_WIDTH].reshape(B, M, MEM_HEADS, MEM_HEAD_DIM)
    mv = kv[..., MEM_WIDTH:].reshape(B, M, MEM_HEADS, MEM_HEAD_DIM)
    return _rmsnorm(mk, xk_norm), mv


def _mem_attn(q, mk, mv):
    s = jnp.einsum('bthd,bmhd->bhtm', q.astype(F32), mk.astype(F32)) * (MEM_HEAD_DIM ** -0.5)
    pr = jax.nn.softmax(s, axis=-1)
    o = jnp.einsum('bhtm,bmhd->bthd', pr, mv.astype(F32))
    return o.astype(q.dtype)


def _layer(x, pos, z_prev0, S0, k_past, v_past, mem_k, mem_v, p):
    B, T, _ = x.shape
    h = _rmsnorm(x, p['norm_mix'])
    proj = h @ p['w_in']
    z, q, k, v, xq, gates = jnp.split(proj, IN_SPLITS, axis=-1)
    z_prev = jnp.concatenate([z_prev0[:, None, :].astype(z.dtype), z[:, :-1]], axis=1)
    out_a, S_new = _rwkv_branch(z, z_prev, S0, p)
    qh = _rope(_rmsnorm(q.reshape(B, T, SWA_HEADS, SWA_HEAD_DIM), p['q_norm']), pos)
    kh = _rope(_rmsnorm(k.reshape(B, T, SWA_KV_HEADS, SWA_HEAD_DIM), p['k_norm']), pos)
    vh = v.reshape(B, T, SWA_KV_HEADS, SWA_HEAD_DIM)
    qh = qh.reshape(B, T, SWA_KV_HEADS, SWA_GROUPS, SWA_HEAD_DIM)
    if k_past is None:
        out_b, k_keep, v_keep = _swa_prompt(qh, kh, vh, p['swa_sinks'])
    else:
        out_b, k_keep, v_keep = _swa_sample(qh, kh, vh, k_past, v_past, p['swa_sinks'])
    xqh = _rmsnorm(xq.reshape(B, T, MEM_HEADS, MEM_HEAD_DIM), p['xq_norm'])
    out_c = _mem_attn(xqh, mem_k, mem_v).reshape(B, T, MEM_WIDTH)
    g = jax.nn.sigmoid(gates)
    merged = (g[..., :D_MODEL] * (out_a @ p['w_br_a'])
              + g[..., D_MODEL:2 * D_MODEL] * (out_b @ p['w_br_b'])
              + g[..., 2 * D_MODEL:] * (out_c @ p['w_br_c']))
    x = x + merged @ p['w_out']
    h2 = _rmsnorm(x, p['norm_ffn'])
    x = x + jnp.square(jax.nn.relu(h2 @ p['w_up'])) @ p['w_down']
    return x, S_new, z[:, -1], k_keep, v_keep


def setup_inputs(seed: int = 0) -> dict:
    key = jax.random.key(seed)
    ks = jax.random.split(key, 36)

    def nrm(i, shape, scale):
        return jax.random.normal(ks[i], shape, F32) * scale

    def gain(i, n):
        return 1.0 + nrm(i, (DEPTH, n), 0.02)

    w_swa = min(WINDOW, PAST_LEN)
    return {
        'x_prompt': nrm(0, (BATCH, SEQ, D_MODEL), 1.0),
        'x_sample': nrm(1, (DEC_BATCH, DEC_SEQ, D_MODEL), 1.0),
        'state_rwkv': nrm(2, (DEPTH, DEC_BATCH, RW_HEADS, RW_HEAD_DIM, RW_HEAD_DIM), 0.3),
        'state_rwkv_shift': nrm(3, (DEPTH, DEC_BATCH, RW_COLS), 1.0),
        'cache_swa_k': nrm(4, (DEPTH, DEC_BATCH, w_swa, SWA_KV_HEADS, SWA_HEAD_DIM), 1.0),
        'cache_swa_v': nrm(5, (DEPTH, DEC_BATCH, w_swa, SWA_KV_HEADS, SWA_HEAD_DIM), 1.0),
        'cache_mem_k': nrm(6, (DEPTH, DEC_BATCH, MEM_LEN, MEM_HEADS, MEM_HEAD_DIM), 1.0),
        'cache_mem_v': nrm(7, (DEPTH, DEC_BATCH, MEM_LEN, MEM_HEADS, MEM_HEAD_DIM), 1.0),
        'mem_prompt': nrm(8, (BATCH, MEM_LEN, D_MODEL), 1.0),
        'norm_mix': gain(9, D_MODEL),
        'w_in': nrm(10, (DEPTH, D_MODEL, IN_COLS), D_MODEL ** -0.5),
        'rw_mu': jax.random.uniform(ks[11], (DEPTH, RW_COLS), F32),
        'rw_w0': -2.0 + nrm(12, (DEPTH, RW_WIDTH), 0.5),
        'rw_w2': nrm(13, (DEPTH, DECAY_LORA, RW_WIDTH), 0.5 * DECAY_LORA ** -0.5),
        'rw_a0': nrm(14, (DEPTH, RW_WIDTH), 0.1),
        'rw_a2': nrm(15, (DEPTH, AAA_LORA, RW_WIDTH), AAA_LORA ** -0.5),
        'rw_g2': nrm(16, (DEPTH, GATE_LORA, RW_WIDTH), GATE_LORA ** -0.5),
        'rw_k_k': 0.85 + nrm(17, (DEPTH, RW_WIDTH), 0.05),
        'rw_k_a': 1.0 + nrm(18, (DEPTH, RW_WIDTH), 0.05),
        'rw_r_k': nrm(19, (DEPTH, RW_HEADS, RW_HEAD_DIM), 0.1),
        'rw_ln_w': gain(20, RW_WIDTH),
        'rw_ln_b': nrm(21, (DEPTH, RW_WIDTH), 0.01),
        'q_norm': gain(22, SWA_HEAD_DIM),
        'k_norm': gain(23, SWA_HEAD_DIM),
        'swa_sinks': nrm(24, (DEPTH, SWA_HEADS), 1.0),
        'mem_norm': gain(25, D_MODEL),
        'w_mem_kv': nrm(26, (DEPTH, D_MODEL, 2 * MEM_WIDTH), D_MODEL ** -0.5),
        'xq_norm': gain(27, MEM_HEAD_DIM),
        'xk_norm': gain(28, MEM_HEAD_DIM),
        'w_br_a': nrm(29, (DEPTH, RW_WIDTH, D_MODEL), RW_WIDTH ** -0.5),
        'w_br_b': nrm(30, (DEPTH, SWA_Q, D_MODEL), SWA_Q ** -0.5),
        'w_br_c': nrm(31, (DEPTH, MEM_WIDTH, D_MODEL), MEM_WIDTH ** -0.5),
        'w_out': nrm(32, (DEPTH, D_MODEL, D_MODEL), D_MODEL ** -0.5),
        'norm_ffn': gain(33, D_MODEL),
        'w_up': nrm(34, (DEPTH, D_MODEL, D_FF), D_MODEL ** -0.5),
        'w_down': nrm(35, (DEPTH, D_FF, D_MODEL), D_FF ** -0.5),
    }


def reference(x_prompt, x_sample, state_rwkv, state_rwkv_shift, cache_swa_k, cache_swa_v,
              cache_mem_k, cache_mem_v, mem_prompt, norm_mix, w_in, rw_mu, rw_w0, rw_w2,
              rw_a0, rw_a2, rw_g2, rw_k_k, rw_k_a, rw_r_k, rw_ln_w, rw_ln_b, q_norm, k_norm,
              swa_sinks, mem_norm, w_mem_kv, xq_norm, xk_norm, w_br_a, w_br_b, w_br_c, w_out,
              norm_ffn, w_up, w_down):
    B, T, _ = x_prompt.shape
    pos_p = jnp.arange(T, dtype=jnp.int32)
    pos_s = PAST_LEN + jnp.arange(x_sample.shape[1], dtype=jnp.int32)
    yp, ys = x_prompt, x_sample
    lp_S, lp_z, lp_k, lp_v, lp_mk, lp_mv = [], [], [], [], [], []
    ls_S, ls_z, ls_k, ls_v = [], [], [], []
    for l in range(DEPTH):
        p = dict(norm_mix=norm_mix[l], w_in=w_in[l], rw_mu=rw_mu[l], rw_w0=rw_w0[l],
                 rw_w2=rw_w2[l], rw_a0=rw_a0[l], rw_a2=rw_a2[l], rw_g2=rw_g2[l],
                 rw_k_k=rw_k_k[l], rw_k_a=rw_k_a[l], rw_r_k=rw_r_k[l], rw_ln_w=rw_ln_w[l],
                 rw_ln_b=rw_ln_b[l], q_norm=q_norm[l], k_norm=k_norm[l], swa_sinks=swa_sinks[l],
                 xq_norm=xq_norm[l], w_br_a=w_br_a[l], w_br_b=w_br_b[l], w_br_c=w_br_c[l],
                 w_out=w_out[l], norm_ffn=norm_ffn[l], w_up=w_up[l], w_down=w_down[l])
        mk_p, mv_p = _mem_kv(mem_prompt, mem_norm[l], w_mem_kv[l], xk_norm[l])
        z0 = jnp.zeros((B, RW_COLS), x_prompt.dtype)
        S0 = jnp.zeros((B, RW_HEADS, RW_HEAD_DIM, RW_HEAD_DIM), F32)
        yp, Sp, zp, kp, vp = _layer(yp, pos_p, z0, S0, None, None, mk_p, mv_p, p)
        ys, Ss, zs_, ks_, vs_ = _layer(ys, pos_s, state_rwkv_shift[l], state_rwkv[l],
                                       cache_swa_k[l], cache_swa_v[l],
                                       cache_mem_k[l], cache_mem_v[l], p)
        lp_S.append(Sp); lp_z.append(zp); lp_k.append(kp); lp_v.append(vp)
        lp_mk.append(mk_p); lp_mv.append(mv_p)
        ls_S.append(Ss); ls_z.append(zs_); ls_k.append(ks_); ls_v.append(vs_)
    y_prompt, y_sample = yp, ys
    new_state_rwkv_prompt = jnp.stack(lp_S)
    new_shift_prompt = jnp.stack(lp_z)
    new_swa_k_prompt = jnp.stack(lp_k)
    new_swa_v_prompt = jnp.stack(lp_v)
    new_mem_k_prompt = jnp.stack(lp_mk)
    new_mem_v_prompt = jnp.stack(lp_mv)
    new_state_rwkv_sample = jnp.stack(ls_S)
    new_shift_sample = jnp.stack(ls_z)
    new_swa_k_sample = jnp.stack(ls_k)
    new_swa_v_sample = jnp.stack(ls_v)
    return (y_prompt, y_sample, new_state_rwkv_prompt, new_shift_prompt, new_swa_k_prompt,
            new_swa_v_prompt, new_mem_k_prompt, new_mem_v_prompt, new_state_rwkv_sample,
            new_shift_sample, new_swa_k_sample, new_swa_v_sample)
```

```python
import functools

import jax
import jax.numpy as jnp
from jax import lax
from jax.experimental import pallas as pl
from jax.experimental.pallas import tpu as pltpu

F32 = jnp.float32
BF16 = jnp.bfloat16

D_MODEL = 1024
RW_HEADS = 8
RW_HEAD_DIM = 64
RW_WIDTH = RW_HEADS * RW_HEAD_DIM
RW_COLS = 3 * RW_WIDTH + 64 + 64 + 128
GN_EPS = 64e-5
SWA_HEADS = 8
SWA_KV_HEADS = 2
SWA_GROUPS = SWA_HEADS // SWA_KV_HEADS
SWA_HEAD_DIM = 64
SWA_Q = SWA_HEADS * SWA_HEAD_DIM
SWA_KV = SWA_KV_HEADS * SWA_HEAD_DIM
WINDOW = 128
PAST_LEN = 8192
ROPE_THETA = 500000.0
ROPE_DIM = SWA_HEAD_DIM // 4
MEM_HEADS = 4
MEM_HEAD_DIM = 128
MEM_WIDTH = MEM_HEADS * MEM_HEAD_DIM
D_FF = 4 * D_MODEL
NORM_EPS = 1e-5
PROJ_COLS = RW_COLS + SWA_Q + 2 * SWA_KV + MEM_WIDTH

LANES = 128
VMEM_LIMIT = 52 * 1024 * 1024


def _params(*sem):
    return pltpu.CompilerParams(dimension_semantics=sem, vmem_limit_bytes=VMEM_LIMIT)


def _const_spec(shape):
    nd = len(shape)
    return pl.BlockSpec(shape, lambda *_: (0,) * nd, pipeline_mode=pl.Buffered(1))


def _whole_spec(shape):
    nd = len(shape)
    return pl.BlockSpec(shape, lambda *_: (0,) * nd)


def _rms(x, gain):
    return x * lax.rsqrt(jnp.mean(x * x, axis=-1, keepdims=True) + NORM_EPS) * gain


def _split(x):
    hi = x.astype(BF16)
    return hi, (x - hi.astype(F32)).astype(BF16)


def _dot(a, b):
    return jnp.dot(a, b, preferred_element_type=F32)


def _segsum(x, ones):
    hi, lo = _split(x)
    return _dot(hi, ones) + _dot(lo, ones)


def _dot3(x, w_hi, w_lo):
    hi, lo = _split(x)
    return _dot(hi, w_hi) + _dot(lo, w_hi) + _dot(hi, w_lo)


def _block_ones(n, seg):
    idx = jnp.arange(n) // seg
    return (idx[:, None] == idx[None, :]).astype(BF16)


def _proj_kernel(x_ref, g_ref, w_ref, z_ref, q_ref, k_ref, v_ref, xq_ref):
    h = _rms(x_ref[...], g_ref[...])
    p = _dot(h.astype(BF16), w_ref[...])
    c0, c1, c2, c3 = RW_COLS, RW_COLS + SWA_Q, RW_COLS + SWA_Q + SWA_KV, RW_COLS + SWA_Q + 2 * SWA_KV
    z_ref[...] = p[:, :c0]
    q_ref[...] = p[:, c0:c1]
    k_ref[...] = p[:, c1:c2]
    v_ref[...] = p[:, c2:c3]
    xq_ref[...] = p[:, c3:]


def _proj(x, gain, w_bf16, tm):
    n = x.shape[0]
    widths = (RW_COLS, SWA_Q, SWA_KV, SWA_KV, MEM_WIDTH)
    return pl.pallas_call(
        _proj_kernel,
        grid=(n // tm,),
        in_specs=[pl.BlockSpec((tm, D_MODEL), lambda i: (i, 0)),
                  _const_spec((1, D_MODEL)),
                  _const_spec((D_MODEL, PROJ_COLS))],
        out_specs=[pl.BlockSpec((tm, w), lambda i: (i, 0)) for w in widths],
        out_shape=[jax.ShapeDtypeStruct((n, w), F32) for w in widths],
        compiler_params=_params("arbitrary"),
        name="in_proj",
    )(x, gain, w_bf16)


def _rwkv_prep_math(z, zprev, mu, w0, w2h, w2l, a0, a2h, a2l, g2h, g2l, k_k, k_a, r_k, ones):
    zs = z + (zprev - z) * mu
    w_ = RW_WIDTH
    r, k, v = zs[:, :w_], zs[:, w_:2 * w_], zs[:, 2 * w_:3 * w_]
    u = zs[:, 3 * w_:3 * w_ + LANES]
    gd = zs[:, 3 * w_ + LANES:]
    w_log = -jax.nn.softplus(-(w0 + _dot3(jnp.tanh(u), w2h, w2l))) - 0.5
    decay = jnp.exp(-jnp.exp(w_log))
    a = jax.nn.sigmoid(a0 + _dot3(u, a2h, a2l))
    g = _dot3(jax.nn.sigmoid(gd), g2h, g2l)
    kkr = k * k_k
    kk = kkr / jnp.maximum(jnp.sqrt(_segsum(kkr * kkr, ones)), 1e-12)
    kh = k * (1.0 + (a - 1.0) * k_a)
    bonus = _segsum(r * kh * r_k, ones) * v
    return kk, decay, kk * a, kh, r, v, g, bonus


def _rwkv_prep_prompt_kernel(z_ref, zb_ref, z0_ref, mu, w0, w2h, w2l, a0, a2h, a2l, g2h, g2l, k_k, k_a, r_k,
                             ones, ops_ref, v_ref, g_ref, bonus_ref):
    z = z_ref[...]
    first = jnp.where(pl.program_id(1) == 0, z0_ref[...], zb_ref[7:8, :])
    row = lax.broadcasted_iota(jnp.int32, z.shape, 0)
    zprev = jnp.where(row == 0, first, pltpu.roll(z, 1, axis=0))
    outs = _rwkv_prep_math(z, zprev, mu[...], w0[...], w2h[...], w2l[...], a0[...], a2h[...], a2l[...],
                           g2h[...], g2l[...], k_k[...], k_a[...], r_k[...], ones[...])
    for j in range(5):
        ops_ref[j] = outs[j]
    v_ref[...] = outs[5]
    g_ref[...] = outs[6]
    bonus_ref[...] = outs[7]


def _rwkv_prep_sample_kernel(z_ref, zp_ref, mu, w0, w2h, w2l, a0, a2h, a2l, g2h, g2l, k_k, k_a, r_k,
                             ones, ops_ref, v_ref, g_ref, bonus_ref):
    outs = _rwkv_prep_math(z_ref[...], zp_ref[...], mu[...], w0[...], w2h[...], w2l[...], a0[...], a2h[...],
                           a2l[...], g2h[...], g2l[...], k_k[...], k_a[...], r_k[...], ones[...])
    for j in range(5):
        ops_ref[j] = outs[j]
    v_ref[...] = outs[5]
    g_ref[...] = outs[6]
    bonus_ref[...] = outs[7]


def _rwkv_prep_prompt(z3, z0, rw, tm):
    bsz, t, _ = z3.shape
    w_ = RW_WIDTH
    row_spec = lambda width: pl.BlockSpec((None, tm, width), lambda b, i: (b, i, 0))
    return pl.pallas_call(
        _rwkv_prep_prompt_kernel,
        grid=(bsz, t // tm),
        in_specs=[row_spec(RW_COLS),
                  pl.BlockSpec((None, 8, RW_COLS), lambda b, i: (b, jnp.maximum(i * (tm // 8) - 1, 0), 0)),
                  pl.BlockSpec((None, 1, RW_COLS), lambda b, i: (b, 0, 0))]
                 + [_const_spec(p.shape) for p in rw],
        out_specs=[pl.BlockSpec((5, None, tm, w_), lambda b, i: (0, b, i, 0)),
                   row_spec(w_), row_spec(w_), row_spec(w_)],
        out_shape=[jax.ShapeDtypeStruct((5, bsz, t, w_), F32)] + [jax.ShapeDtypeStruct((bsz, t, w_), F32)] * 3,
        compiler_params=_params("arbitrary", "arbitrary"),
        name="rwkv_prep_prompt",
    )(z3, z3, z0, *rw)


def _rwkv_prep_sample(z, zprev, rw):
    n = z.shape[0]
    w_ = RW_WIDTH
    return pl.pallas_call(
        _rwkv_prep_sample_kernel,
        grid=(1,),
        in_specs=[_const_spec((n, RW_COLS)), _const_spec((n, RW_COLS))] + [_const_spec(p.shape) for p in rw],
        out_specs=[_whole_spec((5, n, w_)), _whole_spec((n, w_)), _whole_spec((n, w_)), _whole_spec((n, w_))],
        out_shape=[jax.ShapeDtypeStruct((5, n, w_), F32)] + [jax.ShapeDtypeStruct((n, w_), F32)] * 3,
        compiler_params=_params("arbitrary"),
        name="rwkv_prep_sample",
    )(z, zprev, *rw)


SCAN_OPS = 5
SCAN_CHUNK = 32
N_BH = 16


def _tree_sum(parts):
    while len(parts) > 1:
        parts = [parts[i] + parts[i + 1] for i in range(0, len(parts), 2)]
    return parts[0]


def _rwkv_scan_kernel(opp_ref, oppn_ref, vv_ref, s0_ref, y_ref, sout_ref, s_ref, xr0_ref, xr1_ref):
    step = pl.program_id(0)
    tc = vv_ref.shape[0]
    nk = RW_HEAD_DIM

    def build_pair(src_ref, dst_ref, tp):
        for op in range(SCAN_OPS):
            rows = [jnp.broadcast_to(src_ref[op, tp, pl.ds(bh, 1), :], (8, LANES)) for bh in range(N_BH)]
            m_out = jnp.concatenate(rows, axis=0).T
            dst_ref[op, 2 * tp] = m_out[:nk]
            dst_ref[op, 2 * tp + 1] = m_out[nk:]

    @pl.when(step == 0)
    def _():
        s_ref[...] = s0_ref[...]

        def first(tp, carry):
            build_pair(opp_ref, xr0_ref, tp)
            return carry

        lax.fori_loop(0, tc // 2, first, 0)

    def run(cur_ref, nxt_ref):
        def token(t):
            def opnd(op, k):
                return jnp.broadcast_to(cur_ref[op, t, pl.ds(k, 1), :], (8, LANES))

            accs = [None] * 4
            for k in range(nk):
                p = s_ref[k] * opnd(0, k)
                accs[k % 4] = p if accs[k % 4] is None else accs[k % 4] + p
            sa = -_tree_sum(accs)
            vv = vv_ref[t]
            yacc = [None] * 4
            for k in range(nk):
                s_new = s_ref[k] * opnd(1, k) + sa * opnd(2, k) + vv * opnd(3, k)
                s_ref[k] = s_new
                p = s_new * opnd(4, k)
                yacc[k % 4] = p if yacc[k % 4] is None else yacc[k % 4] + p
            y_ref[t] = _tree_sum(yacc)

        def pair(tp, carry):
            build_pair(oppn_ref, nxt_ref, tp)
            token(2 * tp)
            token(2 * tp + 1)
            return carry

        lax.fori_loop(0, tc // 2, pair, 0)

    @pl.when(step % 2 == 0)
    def _():
        run(xr0_ref, xr1_ref)

    @pl.when(step % 2 == 1)
    def _():
        run(xr1_ref, xr0_ref)

    @pl.when(step == pl.num_programs(0) - 1)
    def _():
        sout_ref[...] = s_ref[...]


def _rwkv_scan(opp, vvp, s0p):
    t = vvp.shape[0]
    tc = SCAN_CHUNK
    nsteps = t // tc
    nk = RW_HEAD_DIM
    return pl.pallas_call(
        _rwkv_scan_kernel,
        grid=(nsteps,),
        in_specs=[pl.BlockSpec((SCAN_OPS, tc // 2, N_BH, LANES), lambda i: (0, i, 0, 0)),
                  pl.BlockSpec((SCAN_OPS, tc // 2, N_BH, LANES), lambda i: (0, jnp.minimum(i + 1, nsteps - 1), 0, 0)),
                  pl.BlockSpec((tc, 8, LANES), lambda i: (i, 0, 0)),
                  pl.BlockSpec((nk, 8, LANES), lambda i: (0, 0, 0))],
        out_specs=[pl.BlockSpec((tc, 8, LANES), lambda i: (i, 0, 0)),
                   pl.BlockSpec((nk, 8, LANES), lambda i: (0, 0, 0))],
        out_shape=[jax.ShapeDtypeStruct((t, 8, LANES), F32), jax.ShapeDtypeStruct((nk, 8, LANES), F32)],
        scratch_shapes=[pltpu.VMEM((nk, 8, LANES), F32),
                        pltpu.VMEM((SCAN_OPS, tc, nk, LANES), F32),
                        pltpu.VMEM((SCAN_OPS, tc, nk, LANES), F32)],
        compiler_params=_params("arbitrary"),
        name="rwkv_scan",
    )(opp, opp, vvp, s0p)


def _pack_scan_ops(ops):
    _, bsz, t, _ = ops.shape
    x = ops.reshape(SCAN_OPS, bsz, t // 2, 2, RW_HEADS, RW_HEAD_DIM)
    x = jnp.transpose(x, (0, 2, 1, 4, 3, 5))
    return x.reshape(SCAN_OPS, t // 2, bsz * RW_HEADS, 2 * RW_HEAD_DIM)


def _pack_scan_v(v):
    bsz, t, _ = v.shape
    x = v.reshape(bsz, t, RW_HEADS, 8, 8)
    return jnp.transpose(x, (1, 3, 0, 2, 4)).reshape(t, 8, LANES)


def _unpack_scan_v(yp, bsz):
    t = yp.shape[0]
    x = yp.reshape(t, 8, bsz, RW_HEADS, 8)
    return jnp.transpose(x, (2, 0, 3, 1, 4)).reshape(bsz, t, RW_WIDTH)


def _unpack_scan_state(sp, bsz):
    x = sp.reshape(RW_HEAD_DIM, 8, bsz, RW_HEADS, 8)
    return jnp.transpose(x, (2, 3, 1, 4, 0)).reshape(bsz, RW_HEADS, RW_HEAD_DIM, RW_HEAD_DIM)


def _rwkv_step_kernel(s_ref, ops_ref, v_ref, so_ref, y_ref):
    s = s_ref[...]
    kk, w, b, kh, r = (ops_ref[j] for j in range(SCAN_OPS))
    n = RW_HEAD_DIM
    eye = lax.broadcasted_iota(jnp.int32, (n, n), 0) == lax.broadcasted_iota(jnp.int32, (n, n), 1)
    sa = -jnp.sum(s * kk, axis=-1, keepdims=True)
    vcol = jnp.sum(jnp.where(eye, v_ref[...], 0.0), axis=-1, keepdims=True)
    s_new = s * w + sa * b + vcol * kh
    so_ref[...] = s_new
    ycol = jnp.sum(s_new * r, axis=-1, keepdims=True)
    y_ref[...] = jnp.sum(jnp.where(eye, ycol, 0.0), axis=1, keepdims=True)


def _rwkv_step(state, ops, v, tg):
    nbh, n, _ = state.shape
    return pl.pallas_call(
        _rwkv_step_kernel,
        grid=(nbh // tg,),
        in_specs=[pl.BlockSpec((tg, n, n), lambda i: (i, 0, 0)),
                  pl.BlockSpec((SCAN_OPS, tg, 1, n), lambda i: (0, i, 0, 0)),
                  pl.BlockSpec((tg, 1, n), lambda i: (i, 0, 0))],
        out_specs=[pl.BlockSpec((tg, n, n), lambda i: (i, 0, 0)),
                   pl.BlockSpec((tg, 1, n), lambda i: (i, 0, 0))],
        out_shape=[jax.ShapeDtypeStruct((nbh, n, n), F32), jax.ShapeDtypeStruct((nbh, 1, n), F32)],
        compiler_params=_params("arbitrary"),
        name="rwkv_step",
    )(state, ops, v)


def _norm_rope(x, gain, cosf, s1, s2, ones):
    outs = []
    for j in range(x.shape[1] // LANES):
        xj = x[:, j * LANES:(j + 1) * LANES]
        ms = _segsum(xj * xj, ones) * (1.0 / SWA_HEAD_DIM)
        xn = xj * lax.rsqrt(ms + NORM_EPS) * gain
        outs.append(xn * cosf + pltpu.roll(xn, LANES - ROPE_DIM // 2, axis=1) * s1
                    + pltpu.roll(xn, ROPE_DIM // 2, axis=1) * s2)
    return outs[0] if len(outs) == 1 else jnp.concatenate(outs, axis=1)


def _sink_softmax_pv(s, sink, v_bf16):
    m = jnp.maximum(jnp.max(s, axis=-1, keepdims=True), sink)
    p = jnp.exp(s - m)
    den = jnp.sum(p, axis=-1, keepdims=True) + jnp.exp(sink - m)
    return _dot(p.astype(BF16), v_bf16) / den


def _swa_prompt_kernel(sink_ref, q_ref, k_ref, v_ref, cos_ref, s1_ref, s2_ref, qg_ref, kg_ref, ones_ref,
                       o_ref, kkeep_ref, kprev_ref, vprev_ref):
    i = pl.program_id(1)
    blk = q_ref.shape[0]

    @pl.when(i == 0)
    def _():
        kprev_ref[...] = jnp.zeros_like(kprev_ref)
        vprev_ref[...] = jnp.zeros_like(vprev_ref)

    cosf, s1, s2, ones = cos_ref[...], s1_ref[...], s2_ref[...], ones_ref[...]
    qh = _norm_rope(q_ref[...], qg_ref[...], cosf, s1, s2, ones).astype(BF16)
    kc = _norm_rope(k_ref[...], kg_ref[...], cosf, s1, s2, ones)
    v = v_ref[...]
    kcat = jnp.concatenate([kprev_ref[...], kc], axis=0).astype(BF16)
    vcat = jnp.concatenate([vprev_ref[...], v], axis=0).astype(BF16)
    r = lax.broadcasted_iota(jnp.int32, (blk, 2 * blk), 0)
    c = lax.broadcasted_iota(jnp.int32, (blk, 2 * blk), 1)
    valid = (c >= r) & (c <= r + WINDOW) & ((c >= blk) | (i > 0))
    outs = []
    for h in range(SWA_HEADS):
        g = h // SWA_GROUPS
        kg = kcat[:, g * SWA_HEAD_DIM:(g + 1) * SWA_HEAD_DIM]
        s = lax.dot_general(qh[:, h * SWA_HEAD_DIM:(h + 1) * SWA_HEAD_DIM], kg, (((1,), (1,)), ((), ())),
                            preferred_element_type=F32) * (SWA_HEAD_DIM ** -0.5)
        s = jnp.where(valid, s, -jnp.inf)
        outs.append(_sink_softmax_pv(s, sink_ref[h], vcat[:, g * SWA_HEAD_DIM:(g + 1) * SWA_HEAD_DIM]))
    o_ref[...] = jnp.concatenate(outs, axis=1)
    kprev_ref[...] = kc
    vprev_ref[...] = v

    @pl.when(i == pl.num_programs(1) - 1)
    def _():
        kkeep_ref[...] = kc


def _swa_prompt(q3, k3, v3, tables, qg, kg, sinks, ones):
    bsz, t, _ = q3.shape
    blk = WINDOW
    row = lambda width: pl.BlockSpec((None, blk, width), lambda b, i: (b, i, 0))
    tab = pl.BlockSpec((blk, LANES), lambda b, i: (i, 0))
    return pl.pallas_call(
        _swa_prompt_kernel,
        grid=(bsz, t // blk),
        in_specs=[pl.BlockSpec(memory_space=pltpu.SMEM), row(SWA_Q), row(SWA_KV), row(SWA_KV), tab, tab, tab,
                  _const_spec((1, LANES)), _const_spec((1, LANES)), _const_spec((LANES, LANES))],
        out_specs=[row(SWA_Q), pl.BlockSpec((None, blk, SWA_KV), lambda b, i: (b, 0, 0))],
        out_shape=[jax.ShapeDtypeStruct((bsz, t, SWA_Q), F32), jax.ShapeDtypeStruct((bsz, blk, SWA_KV), F32)],
        scratch_shapes=[pltpu.VMEM((blk, SWA_KV), F32), pltpu.VMEM((blk, SWA_KV), F32)],
        compiler_params=_params("arbitrary", "arbitrary"),
        name="swa_prompt",
    )(sinks, q3, k3, v3, *tables, qg, kg, ones)


def _qk_rope_kernel(q_ref, k_ref, cos_ref, s1_ref, s2_ref, qg_ref, kg_ref, ones_ref, qo_ref, ko_ref):
    cosf, s1, s2, ones = cos_ref[...], s1_ref[...], s2_ref[...], ones_ref[...]
    qo_ref[...] = _norm_rope(q_ref[...], qg_ref[...], cosf, s1, s2, ones)
    ko_ref[...] = _norm_rope(k_ref[...], kg_ref[...], cosf, s1, s2, ones)


def _qk_rope(q, k, tables, qg, kg, ones):
    n = q.shape[0]
    args = (q, k, *tables, qg, kg, ones)
    return pl.pallas_call(
        _qk_rope_kernel,
        grid=(1,),
        in_specs=[_const_spec(a.shape) for a in args],
        out_specs=[_whole_spec((n, SWA_Q)), _whole_spec((n, SWA_KV))],
        out_shape=[jax.ShapeDtypeStruct((n, SWA_Q), F32), jax.ShapeDtypeStruct((n, SWA_KV), F32)],
        compiler_params=_params("arbitrary"),
        name="qk_rope_sample",
    )(*args)


def _swa_sample_kernel(sink_ref, q_ref, kn_ref, vn_ref, ck_ref, cv_ref, o_ref, cko_ref, cvo_ref):
    d = SWA_HEAD_DIM
    for b in range(q_ref.shape[0]):
        qb = q_ref[b]
        kn, vn = kn_ref[pl.ds(b, 1), :], vn_ref[pl.ds(b, 1), :]
        ck, cv = ck_ref[b], cv_ref[b]
        outs = []
        for g in range(SWA_KV_HEADS):
            qg = qb[g * SWA_GROUPS:(g + 1) * SWA_GROUPS, :]
            lanes = slice(g * d, (g + 1) * d)
            s_past = lax.dot_general(qg.astype(BF16), ck[:, lanes].astype(BF16), (((1,), (1,)), ((), ())),
                                     preferred_element_type=F32) * (d ** -0.5)
            s_new = jnp.sum(qg * kn[:, lanes], axis=-1, keepdims=True) * (d ** -0.5)
            hrow = lax.broadcasted_iota(jnp.int32, (SWA_GROUPS, 1), 0)
            sink = jnp.zeros((SWA_GROUPS, 1), F32)
            for j in range(SWA_GROUPS):
                sink = jnp.where(hrow == j, sink_ref[g * SWA_GROUPS + j], sink)
            m = jnp.maximum(jnp.maximum(jnp.max(s_past, axis=-1, keepdims=True), s_new), sink)
            p_past = jnp.exp(s_past - m)
            p_new = jnp.exp(s_new - m)
            den = jnp.sum(p_past, axis=-1, keepdims=True) + p_new + jnp.exp(sink - m)
            o = _dot(p_past.astype(BF16), cv[:, lanes].astype(BF16)) + p_new * vn[:, lanes]
            outs.append(o / den)
        o_ref[b] = jnp.concatenate(outs, axis=0)
        cko_ref[b] = jnp.concatenate([ck[1:], kn], axis=0)
        cvo_ref[b] = jnp.concatenate([cv[1:], vn], axis=0)


def _swa_sample(qh3, kh, v, cache_k, cache_v, sinks, tb):
    n = qh3.shape[0]
    w = cache_k.shape[1]
    cache = pl.BlockSpec((tb, w, SWA_KV), lambda i: (i, 0, 0))
    return pl.pallas_call(
        _swa_sample_kernel,
        grid=(n // tb,),
        in_specs=[pl.BlockSpec(memory_space=pltpu.SMEM),
                  pl.BlockSpec((tb, SWA_HEADS, SWA_HEAD_DIM), lambda i: (i, 0, 0)),
                  pl.BlockSpec((tb, SWA_KV), lambda i: (i, 0)),
                  pl.BlockSpec((tb, SWA_KV), lambda i: (i, 0)),
                  cache, cache],
        out_specs=[pl.BlockSpec((tb, SWA_HEADS, SWA_HEAD_DIM), lambda i: (i, 0, 0)), cache, cache],
        out_shape=[jax.ShapeDtypeStruct((n, SWA_HEADS, SWA_HEAD_DIM), F32),
                   jax.ShapeDtypeStruct(cache_k.shape, F32), jax.ShapeDtypeStruct(cache_v.shape, F32)],
        compiler_params=_params("arbitrary"),
        name="swa_sample",
    )(sinks, qh3, kh, v, cache_k, cache_v)


def _mem_kv_kernel(mem_ref, g_ref, w_ref, kg_ref, mk_ref, mv_ref):
    kv = _dot(_rms(mem_ref[...], g_ref[...]).astype(BF16), w_ref[...])
    kg = kg_ref[...]
    mk_ref[...] = jnp.concatenate(
        [_rms(kv[:, h * MEM_HEAD_DIM:(h + 1) * MEM_HEAD_DIM], kg) for h in range(MEM_HEADS)], axis=1)
    mv_ref[...] = kv[:, MEM_WIDTH:]


def _mem_kv(mem, gain, w_bf16, kgain):
    bsz, m, _ = mem.shape
    out = pl.BlockSpec((None, m, MEM_WIDTH), lambda b: (b, 0, 0))
    return pl.pallas_call(
        _mem_kv_kernel,
        grid=(bsz,),
        in_specs=[pl.BlockSpec((None, m, D_MODEL), lambda b: (b, 0, 0)), _const_spec((1, D_MODEL)),
                  _const_spec((D_MODEL, 2 * MEM_WIDTH)), _const_spec((1, MEM_HEAD_DIM))],
        out_specs=[out, out],
        out_shape=[jax.ShapeDtypeStruct((bsz, m, MEM_WIDTH), F32)] * 2,
        compiler_params=_params("arbitrary"),
        name="mem_kv",
    )(mem, gain, w_bf16, kgain)


def _mem_attn_prompt_kernel(xq_ref, mk_ref, mv_ref, qg_ref, o_ref):
    xq = xq_ref[...]
    qg = qg_ref[...]
    d = MEM_HEAD_DIM
    outs = []
    for h in range(MEM_HEADS):
        lanes = slice(h * d, (h + 1) * d)
        q = _rms(xq[:, lanes], qg).astype(BF16)
        s = lax.dot_general(q, mk_ref[:, lanes].astype(BF16), (((1,), (1,)), ((), ())),
                            preferred_element_type=F32) * (d ** -0.5)
        m = jnp.max(s, axis=-1, keepdims=True)
        p = jnp.exp(s - m)
        den = jnp.sum(p, axis=-1, keepdims=True)
        outs.append(_dot(p.astype(BF16), mv_ref[:, lanes].astype(BF16)) / den)
    o_ref[...] = jnp.concatenate(outs, axis=1)


def _mem_attn_prompt(xq3, mk, mv, qgain, tq):
    bsz, t, _ = xq3.shape
    m = mk.shape[1]
    mem = pl.BlockSpec((None, m, MEM_WIDTH), lambda b, i: (b, 0, 0))
    row = pl.BlockSpec((None, tq, MEM_WIDTH), lambda b, i: (b, i, 0))
    return pl.pallas_call(
        _mem_attn_prompt_kernel,
        grid=(bsz, t // tq),
        in_specs=[row, mem, mem, _const_spec((1, MEM_HEAD_DIM))],
        out_specs=row,
        out_shape=jax.ShapeDtypeStruct((bsz, t, MEM_WIDTH), F32),
        compiler_params=_params("arbitrary", "arbitrary"),
        name="mem_attn_prompt",
    )(xq3, mk, mv, qgain)


def _mem_attn_sample_kernel(xq_ref, mk_ref, mv_ref, qg_ref, o_ref):
    d = MEM_HEAD_DIM
    qg = qg_ref[...]
    for b in range(xq_ref.shape[0]):
        xq = xq_ref[pl.ds(b, 1), :]
        q = jnp.concatenate([_rms(xq[:, h * d:(h + 1) * d], qg) for h in range(MEM_HEADS)], axis=1)
        prod = mk_ref[b] * q
        mv = mv_ref[b]
        outs = []
        for h in range(MEM_HEADS):
            lanes = slice(h * d, (h + 1) * d)
            s = jnp.sum(prod[:, lanes], axis=-1, keepdims=True) * (d ** -0.5)
            p = jnp.exp(s - jnp.max(s, axis=0, keepdims=True))
            den = jnp.sum(p, axis=0, keepdims=True)
            outs.append(jnp.sum(p * mv[:, lanes], axis=0, keepdims=True) / den)
        o_ref[pl.ds(b, 1), :] = jnp.concatenate(outs, axis=1)


def _mem_attn_sample(xq, mk, mv, qgain, tb):
    n, m, _ = mk.shape
    mem = pl.BlockSpec((tb, m, MEM_WIDTH), lambda i: (i, 0, 0))
    row = pl.BlockSpec((tb, MEM_WIDTH), lambda i: (i, 0))
    return pl.pallas_call(
        _mem_attn_sample_kernel,
        grid=(n // tb,),
        in_specs=[row, mem, mem, _const_spec((1, MEM_HEAD_DIM))],
        out_specs=row,
        out_shape=jax.ShapeDtypeStruct((n, MEM_WIDTH), F32),
        compiler_params=_params("arbitrary"),
        name="mem_attn_sample",
    )(xq, mk, mv, qgain)


def _merge_ffn_kernel(x_ref, y_ref, bonus_ref, grw_ref, ob_ref, oc_ref, nmix_ref, wg_ref, lnw_ref, lnb_ref,
                      ones_ref, wa_ref, wb_ref, wc_ref, wo_ref, nffn_ref, wu_ref, wd_ref, out_ref):
    x = x_ref[...]
    gates = jax.nn.sigmoid(_dot(_rms(x, nmix_ref[...]).astype(BF16), wg_ref[...]))
    y = y_ref[...]
    ones = ones_ref[...]
    inv_n = 1.0 / RW_HEAD_DIM
    yc = y - _segsum(y, ones) * inv_n
    var = _segsum(yc * yc, ones) * inv_n
    out_a = (yc * lax.rsqrt(var + GN_EPS) * lnw_ref[...] + lnb_ref[...] + bonus_ref[...]) * grw_ref[...]
    merged = (gates[:, :D_MODEL] * _dot(out_a.astype(BF16), wa_ref[...])
              + gates[:, D_MODEL:2 * D_MODEL] * _dot(ob_ref[...].astype(BF16), wb_ref[...])
              + gates[:, 2 * D_MODEL:] * _dot(oc_ref[...].astype(BF16), wc_ref[...]))
    x1 = x + _dot(merged.astype(BF16), wo_ref[...])
    up = _dot(_rms(x1, nffn_ref[...]).astype(BF16), wu_ref[...])
    act = jnp.square(jnp.maximum(up, 0.0))
    out_ref[...] = x1 + _dot(act.astype(BF16), wd_ref[...])


def _merge_ffn(x, y, bonus, grw, ob, oc, weights, tm):
    n = x.shape[0]
    row = lambda width: pl.BlockSpec((tm, width), lambda i: (i, 0))
    return pl.pallas_call(
        _merge_ffn_kernel,
        grid=(n // tm,),
        in_specs=[row(D_MODEL)] + [row(RW_WIDTH)] * 5 + [_const_spec(w.shape) for w in weights],
        out_specs=row(D_MODEL),
        out_shape=jax.ShapeDtypeStruct((n, D_MODEL), F32),
        compiler_params=_params("arbitrary"),
        name="merge_ffn",
    )(x, y, bonus, grw, ob, oc, *weights)


def _rope_tables(pos):
    half = ROPE_DIM // 2
    inv_freq = jnp.power(jnp.float32(ROPE_THETA), -jnp.arange(half, dtype=F32) * (2.0 / ROPE_DIM))
    ang = pos.astype(F32)[:, None] * inv_freq[None, :]
    cos, sin = jnp.cos(ang), jnp.sin(ang)
    n = pos.shape[0]
    rest = SWA_HEAD_DIM - ROPE_DIM
    z8, zr = jnp.zeros((n, half), F32), jnp.zeros((n, rest), F32)
    cosf = jnp.concatenate([cos, cos, jnp.ones((n, rest), F32)], axis=1)
    s1 = jnp.concatenate([-sin, z8, zr], axis=1)
    s2 = jnp.concatenate([z8, sin, zr], axis=1)
    return tuple(jnp.tile(tbl, (1, LANES // SWA_HEAD_DIM)) for tbl in (cosf, s1, s2))


def kernel(x_prompt, x_sample, state_rwkv, state_rwkv_shift, cache_swa_k, cache_swa_v, cache_mem_k, cache_mem_v, mem_prompt, norm_mix, w_in, rw_mu, rw_w0, rw_w2, rw_a0, rw_a2, rw_g2, rw_k_k, rw_k_a, rw_r_k, rw_ln_w, rw_ln_b, q_norm, k_norm, swa_sinks, mem_norm, w_mem_kv, xq_norm, xk_norm, w_br_a, w_br_b, w_br_c, w_out, norm_ffn, w_up, w_down):
    bsz, t, _ = x_prompt.shape
    nb = x_sample.shape[0]
    assert w_in.shape[0] == 1 and x_sample.shape[1] == 1

    row = lambda p: p.reshape(1, -1)
    w_proj = w_in[0][:, :PROJ_COLS].astype(BF16)
    w_gate = w_in[0][:, PROJ_COLS:].astype(BF16)
    zeros_lora = jnp.zeros((64, RW_WIDTH), F32)
    w2pad = jnp.concatenate([rw_w2[0], zeros_lora], axis=0)
    a2pad = jnp.concatenate([zeros_lora, rw_a2[0]], axis=0)
    hi_lo = lambda w: (w.astype(BF16), (w - w.astype(BF16).astype(F32)).astype(BF16))
    ones_rw = _block_ones(RW_WIDTH, RW_HEAD_DIM)
    rw = (row(rw_mu[0]), row(rw_w0[0]), *hi_lo(w2pad), row(rw_a0[0]), *hi_lo(a2pad), *hi_lo(rw_g2[0]),
          row(rw_k_k[0]), row(rw_k_a[0]), row(rw_r_k[0]), ones_rw)
    qg = jnp.tile(row(q_norm[0]), (1, LANES // SWA_HEAD_DIM))
    kg = jnp.tile(row(k_norm[0]), (1, LANES // SWA_HEAD_DIM))
    ones_swa = _block_ones(LANES, SWA_HEAD_DIM)
    sinks = swa_sinks[0]
    merge_w = (row(norm_mix[0]), w_gate, row(rw_ln_w[0]), row(rw_ln_b[0]), ones_rw,
               w_br_a[0].astype(BF16), w_br_b[0].astype(BF16), w_br_c[0].astype(BF16), w_out[0].astype(BF16),
               row(norm_ffn[0]), w_up[0].astype(BF16), w_down[0].astype(BF16))

    xp = x_prompt.reshape(bsz * t, D_MODEL)
    z, q, k, v, xq = _proj(xp, row(norm_mix[0]), w_proj, 512)
    z3 = z.reshape(bsz, t, RW_COLS)
    ops, vrw, grw, bonus = _rwkv_prep_prompt(z3, jnp.zeros((bsz, 1, RW_COLS), F32), rw, 256)
    yp, sp = _rwkv_scan(_pack_scan_ops(ops), _pack_scan_v(vrw), jnp.zeros((RW_HEAD_DIM, 8, LANES), F32))
    y_rw = _unpack_scan_v(yp, bsz)
    state_p = _unpack_scan_state(sp, bsz)

    tables_p = _rope_tables(jnp.arange(t, dtype=jnp.int32))
    k3, v3 = k.reshape(bsz, t, SWA_KV), v.reshape(bsz, t, SWA_KV)
    out_b, k_keep = _swa_prompt(q.reshape(bsz, t, SWA_Q), k3, v3, tables_p, qg, kg, sinks, ones_swa)
    v_keep = v3[:, t - WINDOW:]

    mk, mv = _mem_kv(mem_prompt, row(mem_norm[0]), w_mem_kv[0].astype(BF16), row(xk_norm[0]))
    out_c = _mem_attn_prompt(xq.reshape(bsz, t, MEM_WIDTH), mk, mv, row(xq_norm[0]), 256)

    flat = lambda a: a.reshape(bsz * t, -1)
    y_prompt = _merge_ffn(xp, flat(y_rw), flat(bonus), flat(grw), flat(out_b), flat(out_c), merge_w, 256)

    xs = x_sample.reshape(nb, D_MODEL)
    zs, qs, ks, vs, xqs = _proj(xs, row(norm_mix[0]), w_proj, nb)
    ops_s, vrw_s, grw_s, bonus_s = _rwkv_prep_sample(zs, state_rwkv_shift[0], rw)
    nbh = nb * RW_HEADS
    state_s, y_s = _rwkv_step(state_rwkv[0].reshape(nbh, RW_HEAD_DIM, RW_HEAD_DIM),
                              ops_s.reshape(SCAN_OPS, nbh, 1, RW_HEAD_DIM),
                              vrw_s.reshape(nbh, 1, RW_HEAD_DIM), 32)

    past = cache_swa_k.shape[2]
    assert past <= WINDOW and past <= PAST_LEN
    tables_s = _rope_tables(jnp.full((1,), PAST_LEN, dtype=jnp.int32))
    qh_s, kh_s = _qk_rope(qs, ks, tables_s, qg, kg, ones_swa)
    ob_s, ck_new, cv_new = _swa_sample(qh_s.reshape(nb, SWA_HEADS, SWA_HEAD_DIM), kh_s, vs,
                                       cache_swa_k[0].reshape(nb, past, SWA_KV),
                                       cache_swa_v[0].reshape(nb, past, SWA_KV), sinks, 8)
    mlen = cache_mem_k.shape[2]
    oc_s = _mem_attn_sample(xqs, cache_mem_k[0].reshape(nb, mlen, MEM_WIDTH),
                            cache_mem_v[0].reshape(nb, mlen, MEM_WIDTH), row(xq_norm[0]), 8)
    y_sample = _merge_ffn(xs, y_s.reshape(nb, RW_WIDTH), bonus_s, grw_s, ob_s.reshape(nb, SWA_Q), oc_s,
                          merge_w, nb)

    kv5 = lambda a, n_: a.reshape(1, n_, -1, SWA_KV_HEADS, SWA_HEAD_DIM)
    mem5 = lambda a: a.reshape(1, bsz, -1, MEM_HEADS, MEM_HEAD_DIM)
    return (y_prompt.reshape(bsz, t, D_MODEL),
            y_sample.reshape(nb, 1, D_MODEL),
            state_p[None],
            z3[:, t - 1][None],
            kv5(k_keep, bsz), kv5(v_keep, bsz),
            mem5(mk), mem5(mv),
            state_s.reshape(1, nb, RW_HEADS, RW_HEAD_DIM, RW_HEAD_DIM),
            zs[None],
            kv5(ck_new, nb), kv5(cv_new, nb))
```

```python
import functools

import jax
import jax.numpy as jnp
from jax import lax
from jax.experimental import pallas as pl
from jax.experimental.pallas import tpu as pltpu

F32 = jnp.float32
BF16 = jnp.bfloat16

D_MODEL = 1024
RW_HEADS = 8
RW_HEAD_DIM = 64
RW_WIDTH = RW_HEADS * RW_HEAD_DIM
RW_COLS = 3 * RW_WIDTH + 64 + 64 + 128
GN_EPS = 64e-5
SWA_HEADS = 8
SWA_KV_HEADS = 2
SWA_GROUPS = SWA_HEADS // SWA_KV_HEADS
SWA_HEAD_DIM = 64
SWA_Q = SWA_HEADS * SWA_HEAD_DIM
SWA_KV = SWA_KV_HEADS * SWA_HEAD_DIM
WINDOW = 128
PAST_LEN = 8192
ROPE_THETA = 500000.0
ROPE_DIM = SWA_HEAD_DIM // 4
MEM_HEADS = 4
MEM_HEAD_DIM = 128
MEM_WIDTH = MEM_HEADS * MEM_HEAD_DIM
D_FF = 4 * D_MODEL
NORM_EPS = 1e-5
PROJ_COLS = RW_COLS + SWA_Q + 2 * SWA_KV + MEM_WIDTH

LANES = 128
VMEM_LIMIT = 52 * 1024 * 1024


def _params(*sem):
    return pltpu.CompilerParams(dimension_semantics=sem, vmem_limit_bytes=VMEM_LIMIT)


def _const_spec(shape):
    nd = len(shape)
    return pl.BlockSpec(shape, lambda *_: (0,) * nd, pipeline_mode=pl.Buffered(1))


def _whole_spec(shape):
    nd = len(shape)
    return pl.BlockSpec(shape, lambda *_: (0,) * nd)


def _rms(x, gain):
    return x * lax.rsqrt(jnp.mean(x * x, axis=-1, keepdims=True) + NORM_EPS) * gain


def _split(x):
    hi = x.astype(BF16)
    return hi, (x - hi.astype(F32)).astype(BF16)


def _dot(a, b):
    return jnp.dot(a, b, preferred_element_type=F32)


def _segsum(x, ones):
    hi, lo = _split(x)
    return _dot(hi, ones) + _dot(lo, ones)


def _dot3(x, w_hi, w_lo):
    hi, lo = _split(x)
    return _dot(hi, w_hi) + _dot(lo, w_hi) + _dot(hi, w_lo)


def _block_ones(n, seg):
    idx = jnp.arange(n) // seg
    return (idx[:, None] == idx[None, :]).astype(BF16)


def _proj_kernel(x_ref, g_ref, w_ref, z_ref, q_ref, k_ref, v_ref, xq_ref):
    h = _rms(x_ref[...], g_ref[...])
    p = _dot(h.astype(BF16), w_ref[...])
    c0, c1, c2, c3 = RW_COLS, RW_COLS + SWA_Q, RW_COLS + SWA_Q + SWA_KV, RW_COLS + SWA_Q + 2 * SWA_KV
    z_ref[...] = p[:, :c0]
    q_ref[...] = p[:, c0:c1]
    k_ref[...] = p[:, c1:c2]
    v_ref[...] = p[:, c2:c3]
    xq_ref[...] = p[:, c3:]


def _proj(x, gain, w_bf16, tm):
    n = x.shape[0]
    widths = (RW_COLS, SWA_Q, SWA_KV, SWA_KV, MEM_WIDTH)
    return pl.pallas_call(
        _proj_kernel,
        grid=(n // tm,),
        in_specs=[pl.BlockSpec((tm, D_MODEL), lambda i: (i, 0)),
                  _const_spec((1, D_MODEL)),
                  _const_spec((D_MODEL, PROJ_COLS))],
        out_specs=[pl.BlockSpec((tm, w), lambda i: (i, 0)) for w in widths],
        out_shape=[jax.ShapeDtypeStruct((n, w), F32) for w in widths],
        compiler_params=_params("arbitrary"),
        name="in_proj",
    )(x, gain, w_bf16)


def _rwkv_prep_math(z, zprev, mu, w0, w2h, w2l, a0, a2h, a2l, g2h, g2l, k_k, k_a, r_k, ones):
    zs = z + (zprev - z) * mu
    w_ = RW_WIDTH
    r, k, v = zs[:, :w_], zs[:, w_:2 * w_], zs[:, 2 * w_:3 * w_]
    u = zs[:, 3 * w_:3 * w_ + LANES]
    gd = zs[:, 3 * w_ + LANES:]
    w_log = -jax.nn.softplus(-(w0 + _dot3(jnp.tanh(u), w2h, w2l))) - 0.5
    decay = jnp.exp(-jnp.exp(w_log))
    a = jax.nn.sigmoid(a0 + _dot3(u, a2h, a2l))
    g = _dot3(jax.nn.sigmoid(gd), g2h, g2l)
    kkr = k * k_k
    kk = kkr / jnp.maximum(jnp.sqrt(_segsum(kkr * kkr, ones)), 1e-12)
    kh = k * (1.0 + (a - 1.0) * k_a)
    bonus = _segsum(r * kh * r_k, ones) * v
    return kk, decay, kk * a, kh, r, v, g, bonus


def _rwkv_prep_prompt_kernel(z_ref, zb_ref, z0_ref, mu, w0, w2h, w2l, a0, a2h, a2l, g2h, g2l, k_k, k_a, r_k,
                             ones, ops_ref, v_ref, g_ref, bonus_ref, nat_ref):
    z = z_ref[...]
    first = jnp.where(pl.program_id(1) == 0, z0_ref[...], zb_ref[7:8, :])
    row = lax.broadcasted_iota(jnp.int32, z.shape, 0)
    zprev = jnp.where(row == 0, first, pltpu.roll(z, 1, axis=0))
    outs = _rwkv_prep_math(z, zprev, mu[...], w0[...], w2h[...], w2l[...], a0[...], a2h[...], a2l[...],
                           g2h[...], g2l[...], k_k[...], k_a[...], r_k[...], ones[...])
    half = z.shape[0] // 2
    left = lax.broadcasted_iota(jnp.int32, (half, LANES), 1) < RW_HEAD_DIM
    for j in range(SCAN_OPS):
        for m in range(RW_WIDTH // LANES):
            nat_ref[m] = outs[j][:, m * LANES:(m + 1) * LANES]
            ge = nat_ref[m, pl.ds(0, half, stride=2), :]
            go = nat_ref[m, pl.ds(1, half, stride=2), :]
            ops_ref[j, 2 * m] = jnp.where(left, ge, pltpu.roll(go, RW_HEAD_DIM, axis=1))
            ops_ref[j, 2 * m + 1] = jnp.where(left, pltpu.roll(ge, RW_HEAD_DIM, axis=1), go)
    v_ref[...] = outs[5]
    g_ref[...] = outs[6]
    bonus_ref[...] = outs[7]


def _rwkv_prep_sample_kernel(z_ref, zp_ref, mu, w0, w2h, w2l, a0, a2h, a2l, g2h, g2l, k_k, k_a, r_k,
                             ones, ops_ref, v_ref, g_ref, bonus_ref):
    outs = _rwkv_prep_math(z_ref[...], zp_ref[...], mu[...], w0[...], w2h[...], w2l[...], a0[...], a2h[...],
                           a2l[...], g2h[...], g2l[...], k_k[...], k_a[...], r_k[...], ones[...])
    for j in range(5):
        ops_ref[j] = outs[j]
    v_ref[...] = outs[5]
    g_ref[...] = outs[6]
    bonus_ref[...] = outs[7]


def _rwkv_prep_prompt(z3, z0, rw, tm):
    bsz, t, _ = z3.shape
    w_ = RW_WIDTH
    row_spec = lambda width: pl.BlockSpec((None, tm, width), lambda b, i: (b, i, 0))
    return pl.pallas_call(
        _rwkv_prep_prompt_kernel,
        grid=(bsz, t // tm),
        in_specs=[row_spec(RW_COLS),
                  pl.BlockSpec((None, 8, RW_COLS), lambda b, i: (b, jnp.maximum(i * (tm // 8) - 1, 0), 0)),
                  pl.BlockSpec((None, 1, RW_COLS), lambda b, i: (b, 0, 0))]
                 + [_const_spec(p.shape) for p in rw],
        out_specs=[pl.BlockSpec((SCAN_OPS, RW_HEADS, tm // 2, LANES), lambda b, i: (0, b, i, 0)),
                   row_spec(w_), row_spec(w_), row_spec(w_)],
        out_shape=[jax.ShapeDtypeStruct((SCAN_OPS, bsz * RW_HEADS, t // 2, LANES), F32)]
                  + [jax.ShapeDtypeStruct((bsz, t, w_), F32)] * 3,
        scratch_shapes=[pltpu.VMEM((w_ // LANES, tm, LANES), F32)],
        compiler_params=_params("arbitrary", "arbitrary"),
        name="rwkv_prep_prompt",
    )(z3, z3, z0, *rw)


def _rwkv_prep_sample(z, zprev, rw):
    n = z.shape[0]
    w_ = RW_WIDTH
    return pl.pallas_call(
        _rwkv_prep_sample_kernel,
        grid=(1,),
        in_specs=[_const_spec((n, RW_COLS)), _const_spec((n, RW_COLS))] + [_const_spec(p.shape) for p in rw],
        out_specs=[_whole_spec((5, n, w_)), _whole_spec((n, w_)), _whole_spec((n, w_)), _whole_spec((n, w_))],
        out_shape=[jax.ShapeDtypeStruct((5, n, w_), F32)] + [jax.ShapeDtypeStruct((n, w_), F32)] * 3,
        compiler_params=_params("arbitrary"),
        name="rwkv_prep_sample",
    )(z, zprev, *rw)


SCAN_OPS = 5
SCAN_CHUNK = 32
N_BH = 16


def _tree_sum(parts):
    while len(parts) > 1:
        parts = [parts[i] + parts[i + 1] for i in range(0, len(parts), 2)]
    return parts[0]


def _rwkv_scan_kernel(opp_ref, oppn_ref, vv_ref, s0_ref, y_ref, sout_ref, s_ref, xr0_ref, xr1_ref):
    step = pl.program_id(0)
    tc = vv_ref.shape[0]
    nk = RW_HEAD_DIM

    def build_pair(src_ref, dst_ref, tp):
        for op in range(SCAN_OPS):
            rows = [jnp.broadcast_to(src_ref[op, bh, pl.ds(tp, 1), :], (8, LANES)) for bh in range(N_BH)]
            m_out = jnp.concatenate(rows, axis=0).T
            dst_ref[op, 2 * tp] = m_out[:nk]
            dst_ref[op, 2 * tp + 1] = m_out[nk:]

    @pl.when(step == 0)
    def _():
        s_ref[...] = s0_ref[...]

        def first(tp, carry):
            build_pair(opp_ref, xr0_ref, tp)
            return carry

        lax.fori_loop(0, tc // 2, first, 0)

    def run(cur_ref, nxt_ref):
        def token(t):
            def opnd(op, k):
                return jnp.broadcast_to(cur_ref[op, t, pl.ds(k, 1), :], (8, LANES))

            accs = [None] * 4
            for k in range(nk):
                p = s_ref[k] * opnd(0, k)
                accs[k % 4] = p if accs[k % 4] is None else accs[k % 4] + p
            sa = -_tree_sum(accs)
            vv = vv_ref[t]
            yacc = [None] * 4
            for k in range(nk):
                s_new = s_ref[k] * opnd(1, k) + sa * opnd(2, k) + vv * opnd(3, k)
                s_ref[k] = s_new
                p = s_new * opnd(4, k)
                yacc[k % 4] = p if yacc[k % 4] is None else yacc[k % 4] + p
            y_ref[t] = _tree_sum(yacc)

        def pair(tp, carry):
            build_pair(oppn_ref, nxt_ref, tp)
            token(2 * tp)
            token(2 * tp + 1)
            return carry

        lax.fori_loop(0, tc // 2, pair, 0)

    @pl.when(step % 2 == 0)
    def _():
        run(xr0_ref, xr1_ref)

    @pl.when(step % 2 == 1)
    def _():
        run(xr1_ref, xr0_ref)

    @pl.when(step == pl.num_programs(0) - 1)
    def _():
        sout_ref[...] = s_ref[...]


def _rwkv_scan(opp, vvp, s0p):
    t = vvp.shape[0]
    tc = SCAN_CHUNK
    nsteps = t // tc
    nk = RW_HEAD_DIM
    return pl.pallas_call(
        _rwkv_scan_kernel,
        grid=(nsteps,),
        in_specs=[pl.BlockSpec((SCAN_OPS, N_BH, tc // 2, LANES), lambda i: (0, 0, i, 0)),
                  pl.BlockSpec((SCAN_OPS, N_BH, tc // 2, LANES), lambda i: (0, 0, jnp.minimum(i + 1, nsteps - 1), 0)),
                  pl.BlockSpec((tc, 8, LANES), lambda i: (i, 0, 0)),
                  pl.BlockSpec((nk, 8, LANES), lambda i: (0, 0, 0))],
        out_specs=[pl.BlockSpec((tc, 8, LANES), lambda i: (i, 0, 0)),
                   pl.BlockSpec((nk, 8, LANES), lambda i: (0, 0, 0))],
        out_shape=[jax.ShapeDtypeStruct((t, 8, LANES), F32), jax.ShapeDtypeStruct((nk, 8, LANES), F32)],
        scratch_shapes=[pltpu.VMEM((nk, 8, LANES), F32),
                        pltpu.VMEM((SCAN_OPS, tc, nk, LANES), F32),
                        pltpu.VMEM((SCAN_OPS, tc, nk, LANES), F32)],
        compiler_params=_params("arbitrary"),
        name="rwkv_scan",
    )(opp, opp, vvp, s0p)


def _pack_scan_v(v):
    bsz, t, _ = v.shape
    x = v.reshape(bsz, t, RW_HEADS, 8, 8)
    return jnp.transpose(x, (1, 3, 0, 2, 4)).reshape(t, 8, LANES)


def _unpack_scan_v(yp, bsz):
    t = yp.shape[0]
    x = yp.reshape(t, 8, bsz, RW_HEADS, 8)
    return jnp.transpose(x, (2, 0, 3, 1, 4)).reshape(bsz, t, RW_WIDTH)


def _unpack_scan_state(sp, bsz):
    x = sp.reshape(RW_HEAD_DIM, 8, bsz, RW_HEADS, 8)
    return jnp.transpose(x, (2, 3, 1, 4, 0)).reshape(bsz, RW_HEADS, RW_HEAD_DIM, RW_HEAD_DIM)


def _rwkv_step_kernel(s_ref, ops_ref, v_ref, so_ref, y_ref):
    s = s_ref[...]
    kk, w, b, kh, r = (ops_ref[j] for j in range(SCAN_OPS))
    n = RW_HEAD_DIM
    eye = lax.broadcasted_iota(jnp.int32, (n, n), 0) == lax.broadcasted_iota(jnp.int32, (n, n), 1)
    sa = -jnp.sum(s * kk, axis=-1, keepdims=True)
    vcol = jnp.sum(jnp.where(eye, v_ref[...], 0.0), axis=-1, keepdims=True)
    s_new = s * w + sa * b + vcol * kh
    so_ref[...] = s_new
    ycol = jnp.sum(s_new * r, axis=-1, keepdims=True)
    y_ref[...] = jnp.sum(jnp.where(eye, ycol, 0.0), axis=1, keepdims=True)


def _rwkv_step(state, ops, v, tg):
    nbh, n, _ = state.shape
    return pl.pallas_call(
        _rwkv_step_kernel,
        grid=(nbh // tg,),
        in_specs=[pl.BlockSpec((tg, n, n), lambda i: (i, 0, 0)),
                  pl.BlockSpec((SCAN_OPS, tg, 1, n), lambda i: (0, i, 0, 0)),
                  pl.BlockSpec((tg, 1, n), lambda i: (i, 0, 0))],
        out_specs=[pl.BlockSpec((tg, n, n), lambda i: (i, 0, 0)),
                   pl.BlockSpec((tg, 1, n), lambda i: (i, 0, 0))],
        out_shape=[jax.ShapeDtypeStruct((nbh, n, n), F32), jax.ShapeDtypeStruct((nbh, 1, n), F32)],
        compiler_params=_params("arbitrary"),
        name="rwkv_step",
    )(state, ops, v)


def _norm_rope(x, gain, cosf, s1, s2, ones):
    outs = []
    for j in range(x.shape[1] // LANES):
        xj = x[:, j * LANES:(j + 1) * LANES]
        ms = _segsum(xj * xj, ones) * (1.0 / SWA_HEAD_DIM)
        xn = xj * lax.rsqrt(ms + NORM_EPS) * gain
        outs.append(xn * cosf + pltpu.roll(xn, LANES - ROPE_DIM // 2, axis=1) * s1
                    + pltpu.roll(xn, ROPE_DIM // 2, axis=1) * s2)
    return outs[0] if len(outs) == 1 else jnp.concatenate(outs, axis=1)


def _sink_softmax_pv(s, sink, v_bf16):
    m = jnp.maximum(jnp.max(s, axis=-1, keepdims=True), sink)
    p = jnp.exp(s - m)
    den = jnp.sum(p, axis=-1, keepdims=True) + jnp.exp(sink - m)
    return _dot(p.astype(BF16), v_bf16) / den


def _swa_prompt_kernel(sink_ref, q_ref, k_ref, v_ref, cos_ref, s1_ref, s2_ref, qg_ref, kg_ref, ones_ref,
                       o_ref, kkeep_ref, kprev_ref, vprev_ref):
    i = pl.program_id(1)
    blk = q_ref.shape[0]

    @pl.when(i == 0)
    def _():
        kprev_ref[...] = jnp.zeros_like(kprev_ref)
        vprev_ref[...] = jnp.zeros_like(vprev_ref)

    cosf, s1, s2, ones = cos_ref[...], s1_ref[...], s2_ref[...], ones_ref[...]
    qh = _norm_rope(q_ref[...], qg_ref[...], cosf, s1, s2, ones).astype(BF16)
    kc = _norm_rope(k_ref[...], kg_ref[...], cosf, s1, s2, ones)
    v = v_ref[...]
    kcat = jnp.concatenate([kprev_ref[...], kc], axis=0).astype(BF16)
    vcat = jnp.concatenate([vprev_ref[...], v], axis=0).astype(BF16)
    r = lax.broadcasted_iota(jnp.int32, (blk, 2 * blk), 0)
    c = lax.broadcasted_iota(jnp.int32, (blk, 2 * blk), 1)
    valid = (c >= r) & (c <= r + WINDOW) & ((c >= blk) | (i > 0))
    outs = []
    for h in range(SWA_HEADS):
        g = h // SWA_GROUPS
        kg = kcat[:, g * SWA_HEAD_DIM:(g + 1) * SWA_HEAD_DIM]
        s = lax.dot_general(qh[:, h * SWA_HEAD_DIM:(h + 1) * SWA_HEAD_DIM], kg, (((1,), (1,)), ((), ())),
                            preferred_element_type=F32) * (SWA_HEAD_DIM ** -0.5)
        s = jnp.where(valid, s, -jnp.inf)
        outs.append(_sink_softmax_pv(s, sink_ref[h], vcat[:, g * SWA_HEAD_DIM:(g + 1) * SWA_HEAD_DIM]))
    o_ref[...] = jnp.concatenate(outs, axis=1)
    kprev_ref[...] = kc
    vprev_ref[...] = v

    @pl.when(i == pl.num_programs(1) - 1)
    def _():
        kkeep_ref[...] = kc


def _swa_prompt(q3, k3, v3, tables, qg, kg, sinks, ones):
    bsz, t, _ = q3.shape
    blk = WINDOW
    row = lambda width: pl.BlockSpec((None, blk, width), lambda b, i: (b, i, 0))
    tab = pl.BlockSpec((blk, LANES), lambda b, i: (i, 0))
    return pl.pallas_call(
        _swa_prompt_kernel,
        grid=(bsz, t // blk),
        in_specs=[pl.BlockSpec(memory_space=pltpu.SMEM), row(SWA_Q), row(SWA_KV), row(SWA_KV), tab, tab, tab,
                  _const_spec((1, LANES)), _const_spec((1, LANES)), _const_spec((LANES, LANES))],
        out_specs=[row(SWA_Q), pl.BlockSpec((None, blk, SWA_KV), lambda b, i: (b, 0, 0))],
        out_shape=[jax.ShapeDtypeStruct((bsz, t, SWA_Q), F32), jax.ShapeDtypeStruct((bsz, blk, SWA_KV), F32)],
        scratch_shapes=[pltpu.VMEM((blk, SWA_KV), F32), pltpu.VMEM((blk, SWA_KV), F32)],
        compiler_params=_params("arbitrary", "arbitrary"),
        name="swa_prompt",
    )(sinks, q3, k3, v3, *tables, qg, kg, ones)


def _qk_rope_kernel(q_ref, k_ref, cos_ref, s1_ref, s2_ref, qg_ref, kg_ref, ones_ref, qo_ref, ko_ref):
    cosf, s1, s2, ones = cos_ref[...], s1_ref[...], s2_ref[...], ones_ref[...]
    qo_ref[...] = _norm_rope(q_ref[...], qg_ref[...], cosf, s1, s2, ones)
    ko_ref[...] = _norm_rope(k_ref[...], kg_ref[...], cosf, s1, s2, ones)


def _qk_rope(q, k, tables, qg, kg, ones):
    n = q.shape[0]
    args = (q, k, *tables, qg, kg, ones)
    return pl.pallas_call(
        _qk_rope_kernel,
        grid=(1,),
        in_specs=[_const_spec(a.shape) for a in args],
        out_specs=[_whole_spec((n, SWA_Q)), _whole_spec((n, SWA_KV))],
        out_shape=[jax.ShapeDtypeStruct((n, SWA_Q), F32), jax.ShapeDtypeStruct((n, SWA_KV), F32)],
        compiler_params=_params("arbitrary"),
        name="qk_rope_sample",
    )(*args)


def _swa_sample_kernel(sink_ref, q_ref, kn_ref, vn_ref, ck_ref, cv_ref, o_ref, cko_ref, cvo_ref):
    d = SWA_HEAD_DIM
    for b in range(q_ref.shape[0]):
        qb = q_ref[b]
        kn, vn = kn_ref[pl.ds(b, 1), :], vn_ref[pl.ds(b, 1), :]
        ck, cv = ck_ref[b], cv_ref[b]
        outs = []
        for g in range(SWA_KV_HEADS):
            qg = qb[g * SWA_GROUPS:(g + 1) * SWA_GROUPS, :]
            lanes = slice(g * d, (g + 1) * d)
            s_past = lax.dot_general(qg.astype(BF16), ck[:, lanes].astype(BF16), (((1,), (1,)), ((), ())),
                                     preferred_element_type=F32) * (d ** -0.5)
            s_new = jnp.sum(qg * kn[:, lanes], axis=-1, keepdims=True) * (d ** -0.5)
            hrow = lax.broadcasted_iota(jnp.int32, (SWA_GROUPS, 1), 0)
            sink = jnp.zeros((SWA_GROUPS, 1), F32)
            for j in range(SWA_GROUPS):
                sink = jnp.where(hrow == j, sink_ref[g * SWA_GROUPS + j], sink)
            m = jnp.maximum(jnp.maximum(jnp.max(s_past, axis=-1, keepdims=True), s_new), sink)
            p_past = jnp.exp(s_past - m)
            p_new = jnp.exp(s_new - m)
            den = jnp.sum(p_past, axis=-1, keepdims=True) + p_new + jnp.exp(sink - m)
            o = _dot(p_past.astype(BF16), cv[:, lanes].astype(BF16)) + p_new * vn[:, lanes]
            outs.append(o / den)
        o_ref[b] = jnp.concatenate(outs, axis=0)
        cko_ref[b] = jnp.concatenate([ck[1:], kn], axis=0)
        cvo_ref[b] = jnp.concatenate([cv[1:], vn], axis=0)


def _swa_sample(qh3, kh, v, cache_k, cache_v, sinks, tb):
    n = qh3.shape[0]
    w = cache_k.shape[1]
    cache = pl.BlockSpec((tb, w, SWA_KV), lambda i: (i, 0, 0))
    return pl.pallas_call(
        _swa_sample_kernel,
        grid=(n // tb,),
        in_specs=[pl.BlockSpec(memory_space=pltpu.SMEM),
                  pl.BlockSpec((tb, SWA_HEADS, SWA_HEAD_DIM), lambda i: (i, 0, 0)),
                  pl.BlockSpec((tb, SWA_KV), lambda i: (i, 0)),
                  pl.BlockSpec((tb, SWA_KV), lambda i: (i, 0)),
                  cache, cache],
        out_specs=[pl.BlockSpec((tb, SWA_HEADS, SWA_HEAD_DIM), lambda i: (i, 0, 0)), cache, cache],
        out_shape=[jax.ShapeDtypeStruct((n, SWA_HEADS, SWA_HEAD_DIM), F32),
                   jax.ShapeDtypeStruct(cache_k.shape, F32), jax.ShapeDtypeStruct(cache_v.shape, F32)],
        compiler_params=_params("arbitrary"),
        name="swa_sample",
    )(sinks, qh3, kh, v, cache_k, cache_v)


def _mem_kv_kernel(mem_ref, g_ref, w_ref, kg_ref, mk_ref, mv_ref):
    kv = _dot(_rms(mem_ref[...], g_ref[...]).astype(BF16), w_ref[...])
    kg = kg_ref[...]
    mk_ref[...] = jnp.concatenate(
        [_rms(kv[:, h * MEM_HEAD_DIM:(h + 1) * MEM_HEAD_DIM], kg) for h in range(MEM_HEADS)], axis=1)
    mv_ref[...] = kv[:, MEM_WIDTH:]


def _mem_kv(mem, gain, w_bf16, kgain):
    bsz, m, _ = mem.shape
    out = pl.BlockSpec((None, m, MEM_WIDTH), lambda b: (b, 0, 0))
    return pl.pallas_call(
        _mem_kv_kernel,
        grid=(bsz,),
        in_specs=[pl.BlockSpec((None, m, D_MODEL), lambda b: (b, 0, 0)), _const_spec((1, D_MODEL)),
                  _const_spec((D_MODEL, 2 * MEM_WIDTH)), _const_spec((1, MEM_HEAD_DIM))],
        out_specs=[out, out],
        out_shape=[jax.ShapeDtypeStruct((bsz, m, MEM_WIDTH), F32)] * 2,
        compiler_params=_params("arbitrary"),
        name="mem_kv",
    )(mem, gain, w_bf16, kgain)


def _mem_attn_prompt_kernel(xq_ref, mk_ref, mv_ref, qg_ref, o_ref):
    xq = xq_ref[...]
    qg = qg_ref[...]
    d = MEM_HEAD_DIM
    outs = []
    for h in range(MEM_HEADS):
        lanes = slice(h * d, (h + 1) * d)
        q = _rms(xq[:, lanes], qg).astype(BF16)
        s = lax.dot_general(q, mk_ref[:, lanes].astype(BF16), (((1,), (1,)), ((), ())),
                            preferred_element_type=F32) * (d ** -0.5)
        m = jnp.max(s, axis=-1, keepdims=True)
        p = jnp.exp(s - m)
        den = jnp.sum(p, axis=-1, keepdims=True)
        outs.append(_dot(p.astype(BF16), mv_ref[:, lanes].astype(BF16)) / den)
    o_ref[...] = jnp.concatenate(outs, axis=1)


def _mem_attn_prompt(xq3, mk, mv, qgain, tq):
    bsz, t, _ = xq3.shape
    m = mk.shape[1]
    mem = pl.BlockSpec((None, m, MEM_WIDTH), lambda b, i: (b, 0, 0))
    row = pl.BlockSpec((None, tq, MEM_WIDTH), lambda b, i: (b, i, 0))
    return pl.pallas_call(
        _mem_attn_prompt_kernel,
        grid=(bsz, t // tq),
        in_specs=[row, mem, mem, _const_spec((1, MEM_HEAD_DIM))],
        out_specs=row,
        out_shape=jax.ShapeDtypeStruct((bsz, t, MEM_WIDTH), F32),
        compiler_params=_params("arbitrary", "arbitrary"),
        name="mem_attn_prompt",
    )(xq3, mk, mv, qgain)


def _mem_attn_sample_kernel(xq_ref, mk_ref, mv_ref, qg_ref, o_ref):
    d = MEM_HEAD_DIM
    qg = qg_ref[...]
    for b in range(xq_ref.shape[0]):
        q = _rms(xq_ref[b], qg)
        s = jnp.sum(mk_ref[b] * q, axis=-1, keepdims=True) * (d ** -0.5)
        p = jnp.exp(s - jnp.max(s, axis=0, keepdims=True))
        o_ref[b] = jnp.sum(p * mv_ref[b], axis=0) / jnp.sum(p, axis=0)


def _mem_attn_sample(xq3, mk5, mv5, qgain, tb):
    _, n, m, _, _ = mk5.shape
    mem = pl.BlockSpec((None, tb, m, MEM_HEADS, MEM_HEAD_DIM), lambda i: (0, i, 0, 0, 0))
    row = pl.BlockSpec((tb, MEM_HEADS, MEM_HEAD_DIM), lambda i: (i, 0, 0))
    return pl.pallas_call(
        _mem_attn_sample_kernel,
        grid=(n // tb,),
        in_specs=[row, mem, mem, _const_spec((1, MEM_HEAD_DIM))],
        out_specs=row,
        out_shape=jax.ShapeDtypeStruct((n, MEM_HEADS, MEM_HEAD_DIM), F32),
        compiler_params=_params("arbitrary"),
        name="mem_attn_sample",
    )(xq3, mk5, mv5, qgain)


def _merge_ffn_kernel(x_ref, y_ref, bonus_ref, grw_ref, ob_ref, oc_ref, nmix_ref, wg_ref, lnw_ref, lnb_ref,
                      ones_ref, wa_ref, wb_ref, wc_ref, wo_ref, nffn_ref, wu_ref, wd_ref, out_ref):
    x = x_ref[...]
    gates = jax.nn.sigmoid(_dot(_rms(x, nmix_ref[...]).astype(BF16), wg_ref[...]))
    y = y_ref[...]
    ones = ones_ref[...]
    inv_n = 1.0 / RW_HEAD_DIM
    yc = y - _segsum(y, ones) * inv_n
    var = _segsum(yc * yc, ones) * inv_n
    out_a = (yc * lax.rsqrt(var + GN_EPS) * lnw_ref[...] + lnb_ref[...] + bonus_ref[...]) * grw_ref[...]
    merged = (gates[:, :D_MODEL] * _dot(out_a.astype(BF16), wa_ref[...])
              + gates[:, D_MODEL:2 * D_MODEL] * _dot(ob_ref[...].astype(BF16), wb_ref[...])
              + gates[:, 2 * D_MODEL:] * _dot(oc_ref[...].astype(BF16), wc_ref[...]))
    x1 = x + _dot(merged.astype(BF16), wo_ref[...])
    up = _dot(_rms(x1, nffn_ref[...]).astype(BF16), wu_ref[...])
    act = jnp.square(jnp.maximum(up, 0.0))
    out_ref[...] = x1 + _dot(act.astype(BF16), wd_ref[...])


def _merge_ffn(x, y, bonus, grw, ob, oc, weights, tm):
    n = x.shape[0]
    row = lambda width: pl.BlockSpec((tm, width), lambda i: (i, 0))
    return pl.pallas_call(
        _merge_ffn_kernel,
        grid=(n // tm,),
        in_specs=[row(D_MODEL)] + [row(RW_WIDTH)] * 5 + [_const_spec(w.shape) for w in weights],
        out_specs=row(D_MODEL),
        out_shape=jax.ShapeDtypeStruct((n, D_MODEL), F32),
        compiler_params=_params("arbitrary"),
        name="merge_ffn",
    )(x, y, bonus, grw, ob, oc, *weights)


def _rope_tables(pos):
    half = ROPE_DIM // 2
    inv_freq = jnp.power(jnp.float32(ROPE_THETA), -jnp.arange(half, dtype=F32) * (2.0 / ROPE_DIM))
    ang = pos.astype(F32)[:, None] * inv_freq[None, :]
    cos, sin = jnp.cos(ang), jnp.sin(ang)
    n = pos.shape[0]
    rest = SWA_HEAD_DIM - ROPE_DIM
    z8, zr = jnp.zeros((n, half), F32), jnp.zeros((n, rest), F32)
    cosf = jnp.concatenate([cos, cos, jnp.ones((n, rest), F32)], axis=1)
    s1 = jnp.concatenate([-sin, z8, zr], axis=1)
    s2 = jnp.concatenate([z8, sin, zr], axis=1)
    return tuple(jnp.tile(tbl, (1, LANES // SWA_HEAD_DIM)) for tbl in (cosf, s1, s2))


def kernel(x_prompt, x_sample, state_rwkv, state_rwkv_shift, cache_swa_k, cache_swa_v, cache_mem_k, cache_mem_v, mem_prompt, norm_mix, w_in, rw_mu, rw_w0, rw_w2, rw_a0, rw_a2, rw_g2, rw_k_k, rw_k_a, rw_r_k, rw_ln_w, rw_ln_b, q_norm, k_norm, swa_sinks, mem_norm, w_mem_kv, xq_norm, xk_norm, w_br_a, w_br_b, w_br_c, w_out, norm_ffn, w_up, w_down):
    bsz, t, _ = x_prompt.shape
    nb = x_sample.shape[0]
    assert w_in.shape[0] == 1 and x_sample.shape[1] == 1

    row = lambda p: p.reshape(1, -1)
    w_proj = w_in[0][:, :PROJ_COLS].astype(BF16)
    w_gate = w_in[0][:, PROJ_COLS:].astype(BF16)
    zeros_lora = jnp.zeros((64, RW_WIDTH), F32)
    w2pad = jnp.concatenate([rw_w2[0], zeros_lora], axis=0)
    a2pad = jnp.concatenate([zeros_lora, rw_a2[0]], axis=0)
    hi_lo = lambda w: (w.astype(BF16), (w - w.astype(BF16).astype(F32)).astype(BF16))
    ones_rw = _block_ones(RW_WIDTH, RW_HEAD_DIM)
    rw = (row(rw_mu[0]), row(rw_w0[0]), *hi_lo(w2pad), row(rw_a0[0]), *hi_lo(a2pad), *hi_lo(rw_g2[0]),
          row(rw_k_k[0]), row(rw_k_a[0]), row(rw_r_k[0]), ones_rw)
    qg = jnp.tile(row(q_norm[0]), (1, LANES // SWA_HEAD_DIM))
    kg = jnp.tile(row(k_norm[0]), (1, LANES // SWA_HEAD_DIM))
    ones_swa = _block_ones(LANES, SWA_HEAD_DIM)
    sinks = swa_sinks[0]
    merge_w = (row(norm_mix[0]), w_gate, row(rw_ln_w[0]), row(rw_ln_b[0]), ones_rw,
               w_br_a[0].astype(BF16), w_br_b[0].astype(BF16), w_br_c[0].astype(BF16), w_out[0].astype(BF16),
               row(norm_ffn[0]), w_up[0].astype(BF16), w_down[0].astype(BF16))

    xp = x_prompt.reshape(bsz * t, D_MODEL)
    z, q, k, v, xq = _proj(xp, row(norm_mix[0]), w_proj, 512)
    z3 = z.reshape(bsz, t, RW_COLS)
    ops, vrw, grw, bonus = _rwkv_prep_prompt(z3, jnp.zeros((bsz, 1, RW_COLS), F32), rw, 256)
    yp, sp = _rwkv_scan(ops, _pack_scan_v(vrw), jnp.zeros((RW_HEAD_DIM, 8, LANES), F32))
    y_rw = _unpack_scan_v(yp, bsz)
    state_p = _unpack_scan_state(sp, bsz)

    tables_p = _rope_tables(jnp.arange(t, dtype=jnp.int32))
    k3, v3 = k.reshape(bsz, t, SWA_KV), v.reshape(bsz, t, SWA_KV)
    out_b, k_keep = _swa_prompt(q.reshape(bsz, t, SWA_Q), k3, v3, tables_p, qg, kg, sinks, ones_swa)
    v_keep = v3[:, t - WINDOW:]

    mk, mv = _mem_kv(mem_prompt, row(mem_norm[0]), w_mem_kv[0].astype(BF16), row(xk_norm[0]))
    out_c = _mem_attn_prompt(xq.reshape(bsz, t, MEM_WIDTH), mk, mv, row(xq_norm[0]), 256)

    flat = lambda a: a.reshape(bsz * t, -1)
    y_prompt = _merge_ffn(xp, flat(y_rw), flat(bonus), flat(grw), flat(out_b), flat(out_c), merge_w, 256)

    xs = x_sample.reshape(nb, D_MODEL)
    zs, qs, ks, vs, xqs = _proj(xs, row(norm_mix[0]), w_proj, nb)
    ops_s, vrw_s, grw_s, bonus_s = _rwkv_prep_sample(zs, state_rwkv_shift[0], rw)
    nbh = nb * RW_HEADS
    state_s, y_s = _rwkv_step(state_rwkv[0].reshape(nbh, RW_HEAD_DIM, RW_HEAD_DIM),
                              ops_s.reshape(SCAN_OPS, nbh, 1, RW_HEAD_DIM),
                              vrw_s.reshape(nbh, 1, RW_HEAD_DIM), 32)

    past = cache_swa_k.shape[2]
    assert past <= WINDOW and past <= PAST_LEN
    tables_s = _rope_tables(jnp.full((1,), PAST_LEN, dtype=jnp.int32))
    qh_s, kh_s = _qk_rope(qs, ks, tables_s, qg, kg, ones_swa)
    ob_s, ck_new, cv_new = _swa_sample(qh_s.reshape(nb, SWA_HEADS, SWA_HEAD_DIM), kh_s, vs,
                                       cache_swa_k[0].reshape(nb, past, SWA_KV),
                                       cache_swa_v[0].reshape(nb, past, SWA_KV), sinks, 8)
    oc_s = _mem_attn_sample(xqs.reshape(nb, MEM_HEADS, MEM_HEAD_DIM), cache_mem_k, cache_mem_v,
                            row(xq_norm[0]), 8)
    y_sample = _merge_ffn(xs, y_s.reshape(nb, RW_WIDTH), bonus_s, grw_s, ob_s.reshape(nb, SWA_Q),
                          oc_s.reshape(nb, MEM_WIDTH), merge_w, nb)

    kv5 = lambda a, n_: a.reshape(1, n_, -1, SWA_KV_HEADS, SWA_HEAD_DIM)
    mem5 = lambda a: a.reshape(1, bsz, -1, MEM_HEADS, MEM_HEAD_DIM)
    return (y_prompt.reshape(bsz, t, D_MODEL),
            y_sample.reshape(nb, 1, D_MODEL),
            state_p[None],
            z3[:, t - 1][None],
            kv5(k_keep, bsz), kv5(v_keep, bsz),
            mem5(mk), mem5(mv),
            state_s.reshape(1, nb, RW_HEADS, RW_HEAD_DIM, RW_HEAD_DIM),
            zs[None],
            kv5(ck_new, nb), kv5(cv_new, nb))
```

```python
import functools

import jax
import jax.numpy as jnp
from jax import lax
from jax.experimental import pallas as pl
from jax.experimental.pallas import tpu as pltpu

F32 = jnp.float32
BF16 = jnp.bfloat16

D_MODEL = 1024
RW_HEADS = 8
RW_HEAD_DIM = 64
RW_WIDTH = RW_HEADS * RW_HEAD_DIM
RW_COLS = 3 * RW_WIDTH + 64 + 64 + 128
GN_EPS = 64e-5
SWA_HEADS = 8
SWA_KV_HEADS = 2
SWA_GROUPS = SWA_HEADS // SWA_KV_HEADS
SWA_HEAD_DIM = 64
SWA_Q = SWA_HEADS * SWA_HEAD_DIM
SWA_KV = SWA_KV_HEADS * SWA_HEAD_DIM
WINDOW = 128
PAST_LEN = 8192
ROPE_THETA = 500000.0
ROPE_DIM = SWA_HEAD_DIM // 4
MEM_HEADS = 4
MEM_HEAD_DIM = 128
MEM_WIDTH = MEM_HEADS * MEM_HEAD_DIM
D_FF = 4 * D_MODEL
NORM_EPS = 1e-5
PROJ_COLS = RW_COLS + SWA_Q + 2 * SWA_KV + MEM_WIDTH

LANES = 128
VMEM_LIMIT = 52 * 1024 * 1024


def _params(*sem):
    return pltpu.CompilerParams(dimension_semantics=sem, vmem_limit_bytes=VMEM_LIMIT)


def _const_spec(shape):
    nd = len(shape)
    return pl.BlockSpec(shape, lambda *_: (0,) * nd, pipeline_mode=pl.Buffered(1))


def _whole_spec(shape):
    nd = len(shape)
    return pl.BlockSpec(shape, lambda *_: (0,) * nd)


def _rms(x, gain):
    return x * lax.rsqrt(jnp.mean(x * x, axis=-1, keepdims=True) + NORM_EPS) * gain


def _split(x):
    hi = x.astype(BF16)
    return hi, (x - hi.astype(F32)).astype(BF16)


def _dot(a, b):
    return jnp.dot(a, b, preferred_element_type=F32)


def _segsum(x, ones):
    hi, lo = _split(x)
    return _dot(hi, ones) + _dot(lo, ones)


def _dot3(x, w_hi, w_lo):
    hi, lo = _split(x)
    return _dot(hi, w_hi) + _dot(lo, w_hi) + _dot(hi, w_lo)


def _block_ones(n, seg):
    idx = jnp.arange(n) // seg
    return (idx[:, None] == idx[None, :]).astype(BF16)


def _proj_kernel(x_ref, g_ref, w_ref, z_ref, q_ref, k_ref, v_ref, xq_ref):
    h = _rms(x_ref[...], g_ref[...])
    p = _dot(h.astype(BF16), w_ref[...])
    c0, c1, c2, c3 = RW_COLS, RW_COLS + SWA_Q, RW_COLS + SWA_Q + SWA_KV, RW_COLS + SWA_Q + 2 * SWA_KV
    z_ref[...] = p[:, :c0]
    q_ref[...] = p[:, c0:c1]
    k_ref[...] = p[:, c1:c2]
    v_ref[...] = p[:, c2:c3]
    xq_ref[...] = p[:, c3:]


def _proj(x, gain, w_bf16, tm):
    n = x.shape[0]
    widths = (RW_COLS, SWA_Q, SWA_KV, SWA_KV, MEM_WIDTH)
    return pl.pallas_call(
        _proj_kernel,
        grid=(n // tm,),
        in_specs=[pl.BlockSpec((tm, D_MODEL), lambda i: (i, 0)),
                  _const_spec((1, D_MODEL)),
                  _const_spec((D_MODEL, PROJ_COLS))],
        out_specs=[pl.BlockSpec((tm, w), lambda i: (i, 0)) for w in widths],
        out_shape=[jax.ShapeDtypeStruct((n, w), F32) for w in widths],
        compiler_params=_params("arbitrary"),
        name="in_proj",
    )(x, gain, w_bf16)


def _rwkv_prep_math(z, zprev, mu, w0, w2h, w2l, a0, a2h, a2l, g2h, g2l, k_k, k_a, r_k, ones, ones_v):
    zs = z + (zprev - z) * mu
    w_ = RW_WIDTH
    r, k, v = zs[:, :w_], zs[:, w_:2 * w_], zs[:, 2 * w_:3 * w_]
    u = zs[:, 3 * w_:3 * w_ + LANES]
    gd = zs[:, 3 * w_ + LANES:]
    w_log = -jax.nn.softplus(-(w0 + _dot3(jnp.tanh(u), w2h, w2l))) - 0.5
    decay = jnp.exp(-jnp.exp(w_log))
    a = jax.nn.sigmoid(a0 + _dot3(u, a2h, a2l))
    g = _dot3(jax.nn.sigmoid(gd), g2h, g2l)
    kkr = k * k_k
    kk = kkr / jnp.maximum(jnp.sqrt(_segsum(kkr * kkr, ones)), 1e-12)
    kh = k * (1.0 + (a - 1.0) * k_a)
    bonus = _segsum(r * kh * r_k, ones_v) * v
    return kk, decay, kk * a, kh, r, v, g, bonus


def _rwkv_prep_prompt_kernel(z_ref, zb_ref, z0_ref, mu, w0, w2h, w2l, a0, a2h, a2l, g2h, g2l, k_k, k_a, r_k,
                             ones, ones_v, ops_ref, vv_ref, g_ref, bonus_ref, nat_ref):
    nbatch, tm, _ = z_ref.shape
    half = tm // 2
    left_half = lax.broadcasted_iota(jnp.int32, (half, LANES), 1) < RW_HEAD_DIM
    left_full = lax.broadcasted_iota(jnp.int32, (tm, LANES), 1) < RW_HEAD_DIM
    row = lax.broadcasted_iota(jnp.int32, (tm, RW_COLS), 0)
    values = []
    for b in range(nbatch):
        z = z_ref[b]
        first = jnp.where(pl.program_id(0) == 0, z0_ref[b], zb_ref[b, 7:8, :])
        zprev = jnp.where(row == 0, first, pltpu.roll(z, 1, axis=0))
        outs = _rwkv_prep_math(z, zprev, mu[...], w0[...], w2h[...], w2l[...], a0[...], a2h[...], a2l[...],
                               g2h[...], g2l[...], k_k[...], k_a[...], r_k[...], ones[...], ones_v[...])
        for j in range(SCAN_OPS):
            for m in range(RW_WIDTH // LANES):
                nat_ref[m] = outs[j][:, m * LANES:(m + 1) * LANES]
                ge = nat_ref[m, pl.ds(0, half, stride=2), :]
                go = nat_ref[m, pl.ds(1, half, stride=2), :]
                head = b * RW_HEADS + 2 * m
                ops_ref[j, head] = jnp.where(left_half, ge, pltpu.roll(go, RW_HEAD_DIM, axis=1))
                ops_ref[j, head + 1] = jnp.where(left_half, pltpu.roll(ge, RW_HEAD_DIM, axis=1), go)
        values.append(outs[5])
        g_ref[b] = outs[6]
        bonus_ref[b] = outs[7]
    assert nbatch == 2
    for vh in range(8):
        m, odd = divmod(vh, 2)
        g0, g1 = (val[:, m * LANES:(m + 1) * LANES] for val in values)
        if odd:
            vv_ref[:, vh, :] = jnp.where(left_full, pltpu.roll(g0, RW_HEAD_DIM, axis=1), g1)
        else:
            vv_ref[:, vh, :] = jnp.where(left_full, g0, pltpu.roll(g1, RW_HEAD_DIM, axis=1))


def _rwkv_prep_sample_kernel(z_ref, zp_ref, mu, w0, w2h, w2l, a0, a2h, a2l, g2h, g2l, k_k, k_a, r_k,
                             ones, ones_v, ops_ref, v_ref, g_ref, bonus_ref):
    outs = _rwkv_prep_math(z_ref[...], zp_ref[...], mu[...], w0[...], w2h[...], w2l[...], a0[...], a2h[...],
                           a2l[...], g2h[...], g2l[...], k_k[...], k_a[...], r_k[...], ones[...], ones_v[...])
    for j in range(5):
        ops_ref[j] = outs[j]
    v_ref[...] = outs[5]
    g_ref[...] = outs[6]
    bonus_ref[...] = outs[7]


def _rwkv_prep_prompt(z3, z0, rw, tm):
    bsz, t, _ = z3.shape
    w_ = RW_WIDTH
    row_spec = lambda width: pl.BlockSpec((bsz, tm, width), lambda i: (0, i, 0))
    return pl.pallas_call(
        _rwkv_prep_prompt_kernel,
        grid=(t // tm,),
        in_specs=[row_spec(RW_COLS),
                  pl.BlockSpec((bsz, 8, RW_COLS), lambda i: (0, jnp.maximum(i * (tm // 8) - 1, 0), 0)),
                  pl.BlockSpec((bsz, 1, RW_COLS), lambda i: (0, 0, 0))]
                 + [_const_spec(p.shape) for p in rw],
        out_specs=[pl.BlockSpec((SCAN_OPS, bsz * RW_HEADS, tm // 2, LANES), lambda i: (0, 0, i, 0)),
                   pl.BlockSpec((tm, 8, LANES), lambda i: (i, 0, 0)),
                   row_spec(w_), row_spec(w_)],
        out_shape=[jax.ShapeDtypeStruct((SCAN_OPS, bsz * RW_HEADS, t // 2, LANES), F32),
                   jax.ShapeDtypeStruct((t, 8, LANES), F32),
                   jax.ShapeDtypeStruct((bsz, t, w_), F32), jax.ShapeDtypeStruct((bsz, t, w_), F32)],
        scratch_shapes=[pltpu.VMEM((w_ // LANES, tm, LANES), F32)],
        compiler_params=_params("arbitrary"),
        name="rwkv_prep_prompt",
    )(z3, z3, z0, *rw)


def _rwkv_prep_sample(z, zprev, rw):
    n = z.shape[0]
    w_ = RW_WIDTH
    return pl.pallas_call(
        _rwkv_prep_sample_kernel,
        grid=(1,),
        in_specs=[_const_spec((n, RW_COLS)), _const_spec((n, RW_COLS))] + [_const_spec(p.shape) for p in rw],
        out_specs=[_whole_spec((5, n, w_)), _whole_spec((n, w_)), _whole_spec((n, w_)), _whole_spec((n, w_))],
        out_shape=[jax.ShapeDtypeStruct((5, n, w_), F32)] + [jax.ShapeDtypeStruct((n, w_), F32)] * 3,
        compiler_params=_params("arbitrary"),
        name="rwkv_prep_sample",
    )(z, zprev, *rw)


SCAN_OPS = 5
SCAN_CHUNK = 32
N_BH = 16


def _tree_sum(parts):
    while len(parts) > 1:
        parts = [parts[i] + parts[i + 1] for i in range(0, len(parts), 2)]
    return parts[0]


def _rwkv_scan_kernel(opp_ref, oppn_ref, vv_ref, s0_ref, y_ref, sout_ref, s_ref, xr0_ref, xr1_ref):
    step = pl.program_id(0)
    tc = vv_ref.shape[0]
    nk = RW_HEAD_DIM

    def build_pair(src_ref, dst_ref, tp):
        for op in range(SCAN_OPS):
            rows = [jnp.broadcast_to(src_ref[op, bh, pl.ds(tp, 1), :], (8, LANES)) for bh in range(N_BH)]
            m_out = jnp.concatenate(rows, axis=0).T
            dst_ref[op, 2 * tp] = m_out[:nk]
            dst_ref[op, 2 * tp + 1] = m_out[nk:]

    @pl.when(step == 0)
    def _():
        s_ref[...] = s0_ref[...]

        def first(tp, carry):
            build_pair(opp_ref, xr0_ref, tp)
            return carry

        lax.fori_loop(0, tc // 2, first, 0)

    def run(cur_ref, nxt_ref):
        def token(t):
            def opnd(op, k):
                return jnp.broadcast_to(cur_ref[op, t, pl.ds(k, 1), :], (8, LANES))

            accs = [None] * 4
            for k in range(nk):
                p = s_ref[k] * opnd(0, k)
                accs[k % 4] = p if accs[k % 4] is None else accs[k % 4] + p
            sa = -_tree_sum(accs)
            vv = vv_ref[t]
            yacc = [None] * 4
            for k in range(nk):
                s_new = s_ref[k] * opnd(1, k) + sa * opnd(2, k) + vv * opnd(3, k)
                s_ref[k] = s_new
                p = s_new * opnd(4, k)
                yacc[k % 4] = p if yacc[k % 4] is None else yacc[k % 4] + p
            y_ref[t] = _tree_sum(yacc)

        def pair(tp, carry):
            build_pair(oppn_ref, nxt_ref, tp)
            token(2 * tp)
            token(2 * tp + 1)
            return carry

        lax.fori_loop(0, tc // 2, pair, 0)

    @pl.when(step % 2 == 0)
    def _():
        run(xr0_ref, xr1_ref)

    @pl.when(step % 2 == 1)
    def _():
        run(xr1_ref, xr0_ref)

    @pl.when(step == pl.num_programs(0) - 1)
    def _():
        sout_ref[...] = s_ref[...]


def _rwkv_scan(opp, vvp, s0p):
    t = vvp.shape[0]
    tc = SCAN_CHUNK
    nsteps = t // tc
    nk = RW_HEAD_DIM
    return pl.pallas_call(
        _rwkv_scan_kernel,
        grid=(nsteps,),
        in_specs=[pl.BlockSpec((SCAN_OPS, N_BH, tc // 2, LANES), lambda i: (0, 0, i, 0)),
                  pl.BlockSpec((SCAN_OPS, N_BH, tc // 2, LANES), lambda i: (0, 0, jnp.minimum(i + 1, nsteps - 1), 0)),
                  pl.BlockSpec((tc, 8, LANES), lambda i: (i, 0, 0)),
                  pl.BlockSpec((nk, 8, LANES), lambda i: (0, 0, 0))],
        out_specs=[pl.BlockSpec((tc, 8, LANES), lambda i: (i, 0, 0)),
                   pl.BlockSpec((nk, 8, LANES), lambda i: (0, 0, 0))],
        out_shape=[jax.ShapeDtypeStruct((t, 8, LANES), F32), jax.ShapeDtypeStruct((nk, 8, LANES), F32)],
        scratch_shapes=[pltpu.VMEM((nk, 8, LANES), F32),
                        pltpu.VMEM((SCAN_OPS, tc, nk, LANES), F32),
                        pltpu.VMEM((SCAN_OPS, tc, nk, LANES), F32)],
        compiler_params=_params("arbitrary"),
        name="rwkv_scan",
    )(opp, opp, vvp, s0p)


def _unpack_scan_state(sp, bsz):
    x = sp.reshape(RW_HEAD_DIM, 8, bsz, RW_HEADS, 8)
    return jnp.transpose(x, (2, 3, 1, 4, 0)).reshape(bsz, RW_HEADS, RW_HEAD_DIM, RW_HEAD_DIM)


def _rwkv_step_kernel(s_ref, ops_ref, v_ref, so_ref, y_ref):
    s = s_ref[...]
    kk, w, b, kh, r = (ops_ref[j] for j in range(SCAN_OPS))
    n = RW_HEAD_DIM
    eye = lax.broadcasted_iota(jnp.int32, (n, n), 0) == lax.broadcasted_iota(jnp.int32, (n, n), 1)
    sa = -jnp.sum(s * kk, axis=-1, keepdims=True)
    vcol = jnp.sum(jnp.where(eye, v_ref[...], 0.0), axis=-1, keepdims=True)
    s_new = s * w + sa * b + vcol * kh
    so_ref[...] = s_new
    ycol = jnp.sum(s_new * r, axis=-1, keepdims=True)
    y_ref[...] = jnp.sum(jnp.where(eye, ycol, 0.0), axis=1, keepdims=True)


def _rwkv_step(state, ops, v, tg):
    nbh, n, _ = state.shape
    return pl.pallas_call(
        _rwkv_step_kernel,
        grid=(nbh // tg,),
        in_specs=[pl.BlockSpec((tg, n, n), lambda i: (i, 0, 0)),
                  pl.BlockSpec((SCAN_OPS, tg, 1, n), lambda i: (0, i, 0, 0)),
                  pl.BlockSpec((tg, 1, n), lambda i: (i, 0, 0))],
        out_specs=[pl.BlockSpec((tg, n, n), lambda i: (i, 0, 0)),
                   pl.BlockSpec((tg, 1, n), lambda i: (i, 0, 0))],
        out_shape=[jax.ShapeDtypeStruct((nbh, n, n), F32), jax.ShapeDtypeStruct((nbh, 1, n), F32)],
        compiler_params=_params("arbitrary"),
        name="rwkv_step",
    )(state, ops, v)


def _norm_rope(x, gain, cosf, s1, s2, ones):
    outs = []
    for j in range(x.shape[1] // LANES):
        xj = x[:, j * LANES:(j + 1) * LANES]
        ms = _segsum(xj * xj, ones) * (1.0 / SWA_HEAD_DIM)
        xn = xj * lax.rsqrt(ms + NORM_EPS) * gain
        outs.append(xn * cosf + pltpu.roll(xn, LANES - ROPE_DIM // 2, axis=1) * s1
                    + pltpu.roll(xn, ROPE_DIM // 2, axis=1) * s2)
    return outs[0] if len(outs) == 1 else jnp.concatenate(outs, axis=1)


def _sink_softmax_pv(s, sink, v_bf16):
    m = jnp.maximum(jnp.max(s, axis=-1, keepdims=True), sink)
    p = jnp.exp(s - m)
    den = jnp.sum(p, axis=-1, keepdims=True) + jnp.exp(sink - m)
    return _dot(p.astype(BF16), v_bf16) / den


def _swa_prompt_kernel(sink_ref, q_ref, k_ref, v_ref, cos_ref, s1_ref, s2_ref, qg_ref, kg_ref, ones_ref,
                       o_ref, kkeep_ref, kprev_ref, vprev_ref):
    i = pl.program_id(1)
    blk = q_ref.shape[0]

    @pl.when(i == 0)
    def _():
        kprev_ref[...] = jnp.zeros_like(kprev_ref)
        vprev_ref[...] = jnp.zeros_like(vprev_ref)

    cosf, s1, s2, ones = cos_ref[...], s1_ref[...], s2_ref[...], ones_ref[...]
    qh = _norm_rope(q_ref[...], qg_ref[...], cosf, s1, s2, ones).astype(BF16)
    kc = _norm_rope(k_ref[...], kg_ref[...], cosf, s1, s2, ones)
    v = v_ref[...]
    kcat = jnp.concatenate([kprev_ref[...], kc], axis=0).astype(BF16)
    vcat = jnp.concatenate([vprev_ref[...], v], axis=0).astype(BF16)
    r = lax.broadcasted_iota(jnp.int32, (blk, 2 * blk), 0)
    c = lax.broadcasted_iota(jnp.int32, (blk, 2 * blk), 1)
    valid = (c >= r) & (c <= r + WINDOW) & ((c >= blk) | (i > 0))
    outs = []
    for h in range(SWA_HEADS):
        g = h // SWA_GROUPS
        kg = kcat[:, g * SWA_HEAD_DIM:(g + 1) * SWA_HEAD_DIM]
        s = lax.dot_general(qh[:, h * SWA_HEAD_DIM:(h + 1) * SWA_HEAD_DIM], kg, (((1,), (1,)), ((), ())),
                            preferred_element_type=F32) * (SWA_HEAD_DIM ** -0.5)
        s = jnp.where(valid, s, -jnp.inf)
        outs.append(_sink_softmax_pv(s, sink_ref[h], vcat[:, g * SWA_HEAD_DIM:(g + 1) * SWA_HEAD_DIM]))
    o_ref[...] = jnp.concatenate(outs, axis=1)
    kprev_ref[...] = kc
    vprev_ref[...] = v

    @pl.when(i == pl.num_programs(1) - 1)
    def _():
        kkeep_ref[...] = kc


def _swa_prompt(q3, k3, v3, tables, qg, kg, sinks, ones):
    bsz, t, _ = q3.shape
    blk = WINDOW
    row = lambda width: pl.BlockSpec((None, blk, width), lambda b, i: (b, i, 0))
    tab = pl.BlockSpec((blk, LANES), lambda b, i: (i, 0))
    return pl.pallas_call(
        _swa_prompt_kernel,
        grid=(bsz, t // blk),
        in_specs=[pl.BlockSpec(memory_space=pltpu.SMEM), row(SWA_Q), row(SWA_KV), row(SWA_KV), tab, tab, tab,
                  _const_spec((1, LANES)), _const_spec((1, LANES)), _const_spec((LANES, LANES))],
        out_specs=[row(SWA_Q), pl.BlockSpec((None, blk, SWA_KV), lambda b, i: (b, 0, 0))],
        out_shape=[jax.ShapeDtypeStruct((bsz, t, SWA_Q), F32), jax.ShapeDtypeStruct((bsz, blk, SWA_KV), F32)],
        scratch_shapes=[pltpu.VMEM((blk, SWA_KV), F32), pltpu.VMEM((blk, SWA_KV), F32)],
        compiler_params=_params("arbitrary", "arbitrary"),
        name="swa_prompt",
    )(sinks, q3, k3, v3, *tables, qg, kg, ones)


def _qk_rope_kernel(q_ref, k_ref, cos_ref, s1_ref, s2_ref, qg_ref, kg_ref, ones_ref, qo_ref, qsw_ref, ko_ref):
    cosf, s1, s2, ones = cos_ref[...], s1_ref[...], s2_ref[...], ones_ref[...]
    qh = _norm_rope(q_ref[...], qg_ref[...], cosf, s1, s2, ones)
    qo_ref[...] = qh
    qsw_ref[...] = jnp.concatenate(
        [pltpu.roll(qh[:, m * LANES:(m + 1) * LANES], SWA_HEAD_DIM, axis=1) for m in range(SWA_Q // LANES)], axis=1)
    ko_ref[...] = _norm_rope(k_ref[...], kg_ref[...], cosf, s1, s2, ones)


def _qk_rope(q, k, tables, qg, kg, ones):
    n = q.shape[0]
    args = (q, k, *tables, qg, kg, ones)
    widths = (SWA_Q, SWA_Q, SWA_KV)
    return pl.pallas_call(
        _qk_rope_kernel,
        grid=(1,),
        in_specs=[_const_spec(a.shape) for a in args],
        out_specs=[_whole_spec((n, w)) for w in widths],
        out_shape=[jax.ShapeDtypeStruct((n, w), F32) for w in widths],
        compiler_params=_params("arbitrary"),
        name="qk_rope_sample",
    )(*args)


def _swa_sample_kernel(sink_ref, q_ref, qsw_ref, kn_ref, vn_ref, ck_ref, cv_ref, o_ref, cko_ref, cvo_ref):
    d = SWA_HEAD_DIM
    scale = d ** -0.5
    lane_group = lax.broadcasted_iota(jnp.int32, (1, SWA_KV), 1) // d
    for b in range(q_ref.shape[0]):
        kn, vn = kn_ref[pl.ds(b, 1), :], vn_ref[pl.ds(b, 1), :]
        ck, cv = ck_ref[b], cv_ref[b]
        pairs = []
        for m in range(SWA_HEADS // 2):
            lanes = slice(m * LANES, (m + 1) * LANES)
            q_nat, q_sw = q_ref[pl.ds(b, 1), lanes], qsw_ref[pl.ds(b, 1), lanes]
            g = (2 * m) // SWA_GROUPS
            halves = []
            for parity in range(2):
                h = 2 * m + parity
                q_at_g = q_nat if parity == g else q_sw
                qrow = jnp.where(lane_group == g, q_at_g, 0.0)
                s_past = jnp.sum(ck * qrow, axis=-1, keepdims=True) * scale
                s_new = jnp.sum(kn * qrow, axis=-1, keepdims=True) * scale
                sink = sink_ref[h]
                mx = jnp.maximum(jnp.maximum(jnp.max(s_past, axis=0, keepdims=True), s_new), sink)
                p_past, p_new = jnp.exp(s_past - mx), jnp.exp(s_new - mx)
                den = jnp.sum(p_past, axis=0, keepdims=True) + p_new + jnp.exp(sink - mx)
                o = (jnp.sum(p_past * cv, axis=0, keepdims=True) + p_new * vn) / den
                halves.append(o if parity == g else pltpu.roll(o, d, axis=1))
            pairs.append(jnp.where(lane_group == 0, halves[0], halves[1]))
        o_ref[pl.ds(b, 1), :] = jnp.concatenate(pairs, axis=1)
        cko_ref[b] = jnp.concatenate([ck[1:], kn], axis=0)
        cvo_ref[b] = jnp.concatenate([cv[1:], vn], axis=0)


def _swa_sample(qh, qh_sw, kh, v, cache_k, cache_v, sinks, tb):
    n = qh.shape[0]
    w = cache_k.shape[1]
    cache = pl.BlockSpec((tb, w, SWA_KV), lambda i: (i, 0, 0))
    row = lambda width: pl.BlockSpec((tb, width), lambda i: (i, 0))
    return pl.pallas_call(
        _swa_sample_kernel,
        grid=(n // tb,),
        in_specs=[pl.BlockSpec(memory_space=pltpu.SMEM), row(SWA_Q), row(SWA_Q), row(SWA_KV), row(SWA_KV),
                  cache, cache],
        out_specs=[row(SWA_Q), cache, cache],
        out_shape=[jax.ShapeDtypeStruct((n, SWA_Q), F32),
                   jax.ShapeDtypeStruct(cache_k.shape, F32), jax.ShapeDtypeStruct(cache_v.shape, F32)],
        compiler_params=_params("arbitrary"),
        name="swa_sample",
    )(sinks, qh, qh_sw, kh, v, cache_k, cache_v)


def _mem_kv_kernel(mem_ref, g_ref, w_ref, kg_ref, mk_ref, mv_ref):
    kv = _dot(_rms(mem_ref[...], g_ref[...]).astype(BF16), w_ref[...])
    kg = kg_ref[...]
    mk_ref[...] = jnp.concatenate(
        [_rms(kv[:, h * MEM_HEAD_DIM:(h + 1) * MEM_HEAD_DIM], kg) for h in range(MEM_HEADS)], axis=1)
    mv_ref[...] = kv[:, MEM_WIDTH:]


def _mem_kv(mem, gain, w_bf16, kgain):
    bsz, m, _ = mem.shape
    out = pl.BlockSpec((None, m, MEM_WIDTH), lambda b: (b, 0, 0))
    return pl.pallas_call(
        _mem_kv_kernel,
        grid=(bsz,),
        in_specs=[pl.BlockSpec((None, m, D_MODEL), lambda b: (b, 0, 0)), _const_spec((1, D_MODEL)),
                  _const_spec((D_MODEL, 2 * MEM_WIDTH)), _const_spec((1, MEM_HEAD_DIM))],
        out_specs=[out, out],
        out_shape=[jax.ShapeDtypeStruct((bsz, m, MEM_WIDTH), F32)] * 2,
        compiler_params=_params("arbitrary"),
        name="mem_kv",
    )(mem, gain, w_bf16, kgain)


def _mem_attn_prompt_kernel(xq_ref, mk_ref, mv_ref, qg_ref, o_ref):
    xq = xq_ref[...]
    qg = qg_ref[...]
    d = MEM_HEAD_DIM
    outs = []
    for h in range(MEM_HEADS):
        lanes = slice(h * d, (h + 1) * d)
        q = _rms(xq[:, lanes], qg).astype(BF16)
        s = lax.dot_general(q, mk_ref[:, lanes].astype(BF16), (((1,), (1,)), ((), ())),
                            preferred_element_type=F32) * (d ** -0.5)
        m = jnp.max(s, axis=-1, keepdims=True)
        p = jnp.exp(s - m)
        den = jnp.sum(p, axis=-1, keepdims=True)
        outs.append(_dot(p.astype(BF16), mv_ref[:, lanes].astype(BF16)) / den)
    o_ref[...] = jnp.concatenate(outs, axis=1)


def _mem_attn_prompt(xq3, mk, mv, qgain, tq):
    bsz, t, _ = xq3.shape
    m = mk.shape[1]
    mem = pl.BlockSpec((None, m, MEM_WIDTH), lambda b, i: (b, 0, 0))
    row = pl.BlockSpec((None, tq, MEM_WIDTH), lambda b, i: (b, i, 0))
    return pl.pallas_call(
        _mem_attn_prompt_kernel,
        grid=(bsz, t // tq),
        in_specs=[row, mem, mem, _const_spec((1, MEM_HEAD_DIM))],
        out_specs=row,
        out_shape=jax.ShapeDtypeStruct((bsz, t, MEM_WIDTH), F32),
        compiler_params=_params("arbitrary", "arbitrary"),
        name="mem_attn_prompt",
    )(xq3, mk, mv, qgain)


def _mem_attn_sample_kernel(xq_ref, mk_ref, mv_ref, qg_ref, o_ref):
    d = MEM_HEAD_DIM
    qg = qg_ref[...]
    for b in range(xq_ref.shape[0]):
        q = _rms(xq_ref[b], qg)
        s = jnp.sum(mk_ref[b] * q, axis=-1, keepdims=True) * (d ** -0.5)
        p = jnp.exp(s - jnp.max(s, axis=0, keepdims=True))
        o_ref[b] = jnp.sum(p * mv_ref[b], axis=0) / jnp.sum(p, axis=0)


def _mem_attn_sample(xq3, mk5, mv5, qgain, tb):
    _, n, m, _, _ = mk5.shape
    mem = pl.BlockSpec((None, tb, m, MEM_HEADS, MEM_HEAD_DIM), lambda i: (0, i, 0, 0, 0))
    row = pl.BlockSpec((tb, MEM_HEADS, MEM_HEAD_DIM), lambda i: (i, 0, 0))
    return pl.pallas_call(
        _mem_attn_sample_kernel,
        grid=(n // tb,),
        in_specs=[row, mem, mem, _const_spec((1, MEM_HEAD_DIM))],
        out_specs=row,
        out_shape=jax.ShapeDtypeStruct((n, MEM_HEADS, MEM_HEAD_DIM), F32),
        compiler_params=_params("arbitrary"),
        name="mem_attn_sample",
    )(xq3, mk5, mv5, qgain)


def _values_from_tiles(y_ref):
    tm = y_ref.shape[0]
    left = lax.broadcasted_iota(jnp.int32, (tm, LANES), 1) < RW_HEAD_DIM
    groups = [[], []]
    for m in range(RW_WIDTH // LANES):
        even, odd = y_ref[:, 2 * m, :], y_ref[:, 2 * m + 1, :]
        groups[0].append(jnp.where(left, even, pltpu.roll(odd, RW_HEAD_DIM, axis=1)))
        groups[1].append(jnp.where(left, pltpu.roll(even, RW_HEAD_DIM, axis=1), odd))
    return jnp.concatenate([jnp.concatenate(g, axis=1) for g in groups], axis=0)


def _merge_ffn_kernel(x_ref, y_ref, bonus_ref, grw_ref, ob_ref, oc_ref, nmix_ref, wg_ref, lnw_ref, lnb_ref,
                      ones_ref, wa_ref, wb_ref, wc_ref, wo_ref, nffn_ref, wu_ref, wd_ref, out_ref, *, y_tiles):
    nbatch, tm, _ = x_ref.shape
    rows = lambda ref: ref[...].reshape(nbatch * tm, ref.shape[-1])
    x = rows(x_ref)
    gates = jax.nn.sigmoid(_dot(_rms(x, nmix_ref[...]).astype(BF16), wg_ref[...]))
    y = _values_from_tiles(y_ref) if y_tiles else rows(y_ref)
    ones = ones_ref[...]
    inv_n = 1.0 / RW_HEAD_DIM
    yc = y - _segsum(y, ones) * inv_n
    var = _segsum(yc * yc, ones) * inv_n
    out_a = (yc * lax.rsqrt(var + GN_EPS) * lnw_ref[...] + lnb_ref[...] + rows(bonus_ref)) * rows(grw_ref)
    merged = (gates[:, :D_MODEL] * _dot(out_a.astype(BF16), wa_ref[...])
              + gates[:, D_MODEL:2 * D_MODEL] * _dot(rows(ob_ref).astype(BF16), wb_ref[...])
              + gates[:, 2 * D_MODEL:] * _dot(rows(oc_ref).astype(BF16), wc_ref[...]))
    x1 = x + _dot(merged.astype(BF16), wo_ref[...])
    up = _dot(_rms(x1, nffn_ref[...]).astype(BF16), wu_ref[...])
    act = jnp.square(jnp.maximum(up, 0.0))
    out_ref[...] = (x1 + _dot(act.astype(BF16), wd_ref[...])).reshape(nbatch, tm, D_MODEL)


def _merge_ffn(x, y, bonus, grw, ob, oc, weights, tm):
    nbatch, t, _ = x.shape
    y_tiles = y.ndim == 3 and y.shape[1:] == (8, LANES) and nbatch == 2
    row = lambda width: pl.BlockSpec((nbatch, tm, width), lambda i: (0, i, 0))
    y_spec = pl.BlockSpec((tm, 8, LANES), lambda i: (i, 0, 0)) if y_tiles else row(RW_WIDTH)
    return pl.pallas_call(
        functools.partial(_merge_ffn_kernel, y_tiles=y_tiles),
        grid=(t // tm,),
        in_specs=[row(D_MODEL), y_spec] + [row(RW_WIDTH)] * 4 + [_const_spec(w.shape) for w in weights],
        out_specs=row(D_MODEL),
        out_shape=jax.ShapeDtypeStruct((nbatch, t, D_MODEL), F32),
        compiler_params=_params("arbitrary"),
        name="merge_ffn",
    )(x, y, bonus, grw, ob, oc, *weights)


def _rope_tables(pos):
    half = ROPE_DIM // 2
    inv_freq = jnp.power(jnp.float32(ROPE_THETA), -jnp.arange(half, dtype=F32) * (2.0 / ROPE_DIM))
    ang = pos.astype(F32)[:, None] * inv_freq[None, :]
    cos, sin = jnp.cos(ang), jnp.sin(ang)
    n = pos.shape[0]
    rest = SWA_HEAD_DIM - ROPE_DIM
    z8, zr = jnp.zeros((n, half), F32), jnp.zeros((n, rest), F32)
    cosf = jnp.concatenate([cos, cos, jnp.ones((n, rest), F32)], axis=1)
    s1 = jnp.concatenate([-sin, z8, zr], axis=1)
    s2 = jnp.concatenate([z8, sin, zr], axis=1)
    return tuple(jnp.tile(tbl, (1, LANES // SWA_HEAD_DIM)) for tbl in (cosf, s1, s2))


def kernel(x_prompt, x_sample, state_rwkv, state_rwkv_shift, cache_swa_k, cache_swa_v, cache_mem_k, cache_mem_v, mem_prompt, norm_mix, w_in, rw_mu, rw_w0, rw_w2, rw_a0, rw_a2, rw_g2, rw_k_k, rw_k_a, rw_r_k, rw_ln_w, rw_ln_b, q_norm, k_norm, swa_sinks, mem_norm, w_mem_kv, xq_norm, xk_norm, w_br_a, w_br_b, w_br_c, w_out, norm_ffn, w_up, w_down):
    bsz, t, _ = x_prompt.shape
    nb = x_sample.shape[0]
    assert w_in.shape[0] == 1 and x_sample.shape[1] == 1

    row = lambda p: p.reshape(1, -1)
    w_ = RW_WIDTH
    vcols = slice(2 * w_, 3 * w_)

    def tile_order(a, axis=-1):
        a = jnp.moveaxis(a, axis, -1)
        a = a.reshape(a.shape[:-1] + (RW_HEADS, 8, 8)).swapaxes(-3, -2).reshape(a.shape)
        return jnp.moveaxis(a, -1, axis)

    def tile_order_vcols(a):
        return jnp.concatenate([a[..., :2 * w_], tile_order(a[..., vcols]), a[..., 3 * w_:]], axis=-1)

    w_proj = tile_order_vcols(w_in[0][:, :PROJ_COLS]).astype(BF16)
    w_gate = w_in[0][:, PROJ_COLS:].astype(BF16)
    zeros_lora = jnp.zeros((64, w_), F32)
    w2pad = jnp.concatenate([rw_w2[0], zeros_lora], axis=0)
    a2pad = jnp.concatenate([zeros_lora, rw_a2[0]], axis=0)
    hi_lo = lambda w: (w.astype(BF16), (w - w.astype(BF16).astype(F32)).astype(BF16))
    head_nat = jnp.arange(w_) // RW_HEAD_DIM
    head_tile = (jnp.arange(w_) // 8) % RW_HEADS
    ones_kk = (head_nat[:, None] == head_nat[None, :]).astype(BF16)
    ones_kv = (head_nat[:, None] == head_tile[None, :]).astype(BF16)
    ones_vv = (head_tile[:, None] == head_tile[None, :]).astype(BF16)
    rw = (row(tile_order_vcols(rw_mu[0])), row(rw_w0[0]), *hi_lo(w2pad), row(rw_a0[0]), *hi_lo(a2pad),
          *hi_lo(tile_order(rw_g2[0])), row(rw_k_k[0]), row(rw_k_a[0]), row(rw_r_k[0]), ones_kk, ones_kv)
    qg = jnp.tile(row(q_norm[0]), (1, LANES // SWA_HEAD_DIM))
    kg = jnp.tile(row(k_norm[0]), (1, LANES // SWA_HEAD_DIM))
    ones_swa = _block_ones(LANES, SWA_HEAD_DIM)
    sinks = swa_sinks[0]
    merge_w = (row(norm_mix[0]), w_gate, row(tile_order(rw_ln_w[0])), row(tile_order(rw_ln_b[0])), ones_vv,
               tile_order(w_br_a[0], axis=0).astype(BF16), w_br_b[0].astype(BF16), w_br_c[0].astype(BF16),
               w_out[0].astype(BF16), row(norm_ffn[0]), w_up[0].astype(BF16), w_down[0].astype(BF16))

    xp = x_prompt.reshape(bsz * t, D_MODEL)
    z, q, k, v, xq = _proj(xp, row(norm_mix[0]), w_proj, 512)
    z3 = z.reshape(bsz, t, RW_COLS)
    ops, vv, grw, bonus = _rwkv_prep_prompt(z3, jnp.zeros((bsz, 1, RW_COLS), F32), rw, 256)
    yp, sp = _rwkv_scan(ops, vv, jnp.zeros((RW_HEAD_DIM, 8, LANES), F32))
    state_p = _unpack_scan_state(sp, bsz)

    tables_p = _rope_tables(jnp.arange(t, dtype=jnp.int32))
    k3, v3 = k.reshape(bsz, t, SWA_KV), v.reshape(bsz, t, SWA_KV)
    out_b, k_keep = _swa_prompt(q.reshape(bsz, t, SWA_Q), k3, v3, tables_p, qg, kg, sinks, ones_swa)
    v_keep = v3[:, t - WINDOW:]

    mk, mv = _mem_kv(mem_prompt, row(mem_norm[0]), w_mem_kv[0].astype(BF16), row(xk_norm[0]))
    out_c = _mem_attn_prompt(xq.reshape(bsz, t, MEM_WIDTH), mk, mv, row(xq_norm[0]), 256)

    y_prompt = _merge_ffn(x_prompt, yp, bonus, grw, out_b, out_c, merge_w, 128)

    xs = x_sample.reshape(nb, D_MODEL)
    zs, qs, ks, vs, xqs = _proj(xs, row(norm_mix[0]), w_proj, nb)
    ops_s, vrw_s, grw_s, bonus_s = _rwkv_prep_sample(zs, tile_order_vcols(state_rwkv_shift[0]), rw)
    nbh = nb * RW_HEADS
    state_s, y_s = _rwkv_step(state_rwkv[0].reshape(nbh, RW_HEAD_DIM, RW_HEAD_DIM),
                              ops_s.reshape(SCAN_OPS, nbh, 1, RW_HEAD_DIM),
                              tile_order(vrw_s).reshape(nbh, 1, RW_HEAD_DIM), 32)

    past = cache_swa_k.shape[2]
    assert past <= WINDOW and past <= PAST_LEN
    tables_s = _rope_tables(jnp.full((1,), PAST_LEN, dtype=jnp.int32))
    qh_s, qh_sw, kh_s = _qk_rope(qs, ks, tables_s, qg, kg, ones_swa)
    ob_s, ck_new, cv_new = _swa_sample(qh_s, qh_sw, kh_s, vs, cache_swa_k[0].reshape(nb, past, SWA_KV),
                                       cache_swa_v[0].reshape(nb, past, SWA_KV), sinks, 8)
    oc_s = _mem_attn_sample(xqs.reshape(nb, MEM_HEADS, MEM_HEAD_DIM), cache_mem_k, cache_mem_v,
                            row(xq_norm[0]), 8)
    y_sample = _merge_ffn(xs[None], tile_order(y_s.reshape(nb, w_))[None], bonus_s[None], grw_s[None],
                          ob_s.reshape(1, nb, SWA_Q), oc_s.reshape(1, nb, MEM_WIDTH), merge_w, nb)

    kv5 = lambda a, n_: a.reshape(1, n_, -1, SWA_KV_HEADS, SWA_HEAD_DIM)
    mem5 = lambda a: a.reshape(1, bsz, -1, MEM_HEADS, MEM_HEAD_DIM)
    return (y_prompt,
            y_sample.reshape(nb, 1, D_MODEL),
            state_p[None],
            tile_order_vcols(z3[:, t - 1])[None],
            kv5(k_keep, bsz), kv5(v_keep, bsz),
            mem5(mk), mem5(mv),
            state_s.reshape(1, nb, RW_HEADS, RW_HEAD_DIM, RW_HEAD_DIM),
            tile_order_vcols(zs)[None],
            kv5(ck_new, nb), kv5(cv_new, nb))
```

```python
import functools

import jax
import jax.numpy as jnp
from jax import lax
from jax.experimental import pallas as pl
from jax.experimental.pallas import tpu as pltpu

F32 = jnp.float32
BF16 = jnp.bfloat16

D_MODEL = 1024
RW_HEADS = 8
RW_HEAD_DIM = 64
RW_WIDTH = RW_HEADS * RW_HEAD_DIM
RW_COLS = 3 * RW_WIDTH + 64 + 64 + 128
GN_EPS = 64e-5
SWA_HEADS = 8
SWA_KV_HEADS = 2
SWA_GROUPS = SWA_HEADS // SWA_KV_HEADS
SWA_HEAD_DIM = 64
SWA_Q = SWA_HEADS * SWA_HEAD_DIM
SWA_KV = SWA_KV_HEADS * SWA_HEAD_DIM
WINDOW = 128
PAST_LEN = 8192
ROPE_THETA = 500000.0
ROPE_DIM = SWA_HEAD_DIM // 4
MEM_HEADS = 4
MEM_HEAD_DIM = 128
MEM_WIDTH = MEM_HEADS * MEM_HEAD_DIM
D_FF = 4 * D_MODEL
NORM_EPS = 1e-5
PROJ_COLS = RW_COLS + SWA_Q + 2 * SWA_KV + MEM_WIDTH

LANES = 128
VMEM_LIMIT = 52 * 1024 * 1024


def _params(*sem):
    return pltpu.CompilerParams(dimension_semantics=sem, vmem_limit_bytes=VMEM_LIMIT)


def _const_spec(shape):
    nd = len(shape)
    return pl.BlockSpec(shape, lambda *_: (0,) * nd, pipeline_mode=pl.Buffered(1))


def _whole_spec(shape):
    nd = len(shape)
    return pl.BlockSpec(shape, lambda *_: (0,) * nd)


def _rms(x, gain):
    return x * lax.rsqrt(jnp.mean(x * x, axis=-1, keepdims=True) + NORM_EPS) * gain


def _split(x):
    hi = x.astype(BF16)
    return hi, (x - hi.astype(F32)).astype(BF16)


def _dot(a, b):
    return jnp.dot(a, b, preferred_element_type=F32)


def _segsum(x, ones):
    hi, lo = _split(x)
    return _dot(hi, ones) + _dot(lo, ones)


def _dot3(x, w_hi, w_lo):
    hi, lo = _split(x)
    return _dot(hi, w_hi) + _dot(lo, w_hi) + _dot(hi, w_lo)


def _block_ones(n, seg):
    idx = jnp.arange(n) // seg
    return (idx[:, None] == idx[None, :]).astype(BF16)


def _proj_kernel(x_ref, g_ref, w_ref, z_ref, q_ref, k_ref, v_ref, xq_ref):
    h = _rms(x_ref[...], g_ref[...])
    p = _dot(h.astype(BF16), w_ref[...])
    c0, c1, c2, c3 = RW_COLS, RW_COLS + SWA_Q, RW_COLS + SWA_Q + SWA_KV, RW_COLS + SWA_Q + 2 * SWA_KV
    z_ref[...] = p[:, :c0]
    q_ref[...] = p[:, c0:c1]
    k_ref[...] = p[:, c1:c2]
    v_ref[...] = p[:, c2:c3]
    xq_ref[...] = p[:, c3:]


def _proj(x, gain, w_bf16, tm):
    n = x.shape[0]
    widths = (RW_COLS, SWA_Q, SWA_KV, SWA_KV, MEM_WIDTH)
    return pl.pallas_call(
        _proj_kernel,
        grid=(n // tm,),
        in_specs=[pl.BlockSpec((tm, D_MODEL), lambda i: (i, 0)),
                  _const_spec((1, D_MODEL)),
                  _const_spec((D_MODEL, PROJ_COLS))],
        out_specs=[pl.BlockSpec((tm, w), lambda i: (i, 0)) for w in widths],
        out_shape=[jax.ShapeDtypeStruct((n, w), F32) for w in widths],
        compiler_params=_params("arbitrary"),
        name="in_proj",
    )(x, gain, w_bf16)


def _rwkv_prep_math(z, zprev, mu, w0, w2h, w2l, a0, a2h, a2l, g2h, g2l, k_k, k_a, r_k, ones, ones_v):
    zs = z + (zprev - z) * mu
    w_ = RW_WIDTH
    r, k, v = zs[:, :w_], zs[:, w_:2 * w_], zs[:, 2 * w_:3 * w_]
    u = zs[:, 3 * w_:3 * w_ + LANES]
    gd = zs[:, 3 * w_ + LANES:]
    w_log = -jax.nn.softplus(-(w0 + _dot3(jnp.tanh(u), w2h, w2l))) - 0.5
    decay = jnp.exp(-jnp.exp(w_log))
    a = jax.nn.sigmoid(a0 + _dot3(u, a2h, a2l))
    g = _dot3(jax.nn.sigmoid(gd), g2h, g2l)
    kkr = k * k_k
    kk = kkr / jnp.maximum(jnp.sqrt(_segsum(kkr * kkr, ones)), 1e-12)
    kh = k * (1.0 + (a - 1.0) * k_a)
    bonus = _segsum(r * kh * r_k, ones_v) * v
    return kk, decay, kk * a, kh, r, v, g, bonus


def _rwkv_prep_prompt_kernel(z_ref, zb_ref, z0_ref, mu, w0, w2h, w2l, a0, a2h, a2l, g2h, g2l, k_k, k_a, r_k,
                             ones, ones_v, ops_ref, vv_ref, g_ref, bonus_ref, nat_ref):
    nbatch, tm, _ = z_ref.shape
    half = tm // 2
    left_half = lax.broadcasted_iota(jnp.int32, (half, LANES), 1) < RW_HEAD_DIM
    left_full = lax.broadcasted_iota(jnp.int32, (tm, LANES), 1) < RW_HEAD_DIM
    row = lax.broadcasted_iota(jnp.int32, (tm, RW_COLS), 0)
    values = []
    for b in range(nbatch):
        z = z_ref[b]
        first = jnp.where(pl.program_id(0) == 0, z0_ref[b], zb_ref[b, 7:8, :])
        zprev = jnp.where(row == 0, first, pltpu.roll(z, 1, axis=0))
        outs = _rwkv_prep_math(z, zprev, mu[...], w0[...], w2h[...], w2l[...], a0[...], a2h[...], a2l[...],
                               g2h[...], g2l[...], k_k[...], k_a[...], r_k[...], ones[...], ones_v[...])
        for j in range(SCAN_OPS):
            for m in range(RW_WIDTH // LANES):
                nat_ref[m] = outs[j][:, m * LANES:(m + 1) * LANES]
                ge = nat_ref[m, pl.ds(0, half, stride=2), :]
                go = nat_ref[m, pl.ds(1, half, stride=2), :]
                head = b * RW_HEADS + 2 * m
                ops_ref[j, head] = jnp.where(left_half, ge, pltpu.roll(go, RW_HEAD_DIM, axis=1))
                ops_ref[j, head + 1] = jnp.where(left_half, pltpu.roll(ge, RW_HEAD_DIM, axis=1), go)
        values.append(outs[5])
        g_ref[b] = outs[6]
        bonus_ref[b] = outs[7]
    assert nbatch == 2
    for vh in range(8):
        m, odd = divmod(vh, 2)
        g0, g1 = (val[:, m * LANES:(m + 1) * LANES] for val in values)
        if odd:
            vv_ref[:, vh, :] = jnp.where(left_full, pltpu.roll(g0, RW_HEAD_DIM, axis=1), g1)
        else:
            vv_ref[:, vh, :] = jnp.where(left_full, g0, pltpu.roll(g1, RW_HEAD_DIM, axis=1))


def _rwkv_prep_sample_kernel(z_ref, zp_ref, mu, w0, w2h, w2l, a0, a2h, a2l, g2h, g2l, k_k, k_a, r_k,
                             ones, ones_v, ops_ref, v_ref, g_ref, bonus_ref):
    outs = _rwkv_prep_math(z_ref[...], zp_ref[...], mu[...], w0[...], w2h[...], w2l[...], a0[...], a2h[...],
                           a2l[...], g2h[...], g2l[...], k_k[...], k_a[...], r_k[...], ones[...], ones_v[...])
    for j in range(5):
        ops_ref[j] = outs[j]
    v_ref[...] = outs[5]
    g_ref[...] = outs[6]
    bonus_ref[...] = outs[7]


def _rwkv_prep_prompt(z3, z0, rw, tm):
    bsz, t, _ = z3.shape
    w_ = RW_WIDTH
    row_spec = lambda width: pl.BlockSpec((bsz, tm, width), lambda i: (0, i, 0))
    return pl.pallas_call(
        _rwkv_prep_prompt_kernel,
        grid=(t // tm,),
        in_specs=[row_spec(RW_COLS),
                  pl.BlockSpec((bsz, 8, RW_COLS), lambda i: (0, jnp.maximum(i * (tm // 8) - 1, 0), 0)),
                  pl.BlockSpec((bsz, 1, RW_COLS), lambda i: (0, 0, 0))]
                 + [_const_spec(p.shape) for p in rw],
        out_specs=[pl.BlockSpec((SCAN_OPS, bsz * RW_HEADS, tm // 2, LANES), lambda i: (0, 0, i, 0)),
                   pl.BlockSpec((tm, 8, LANES), lambda i: (i, 0, 0)),
                   row_spec(w_), row_spec(w_)],
        out_shape=[jax.ShapeDtypeStruct((SCAN_OPS, bsz * RW_HEADS, t // 2, LANES), F32),
                   jax.ShapeDtypeStruct((t, 8, LANES), F32),
                   jax.ShapeDtypeStruct((bsz, t, w_), F32), jax.ShapeDtypeStruct((bsz, t, w_), F32)],
        scratch_shapes=[pltpu.VMEM((w_ // LANES, tm, LANES), F32)],
        compiler_params=_params("arbitrary"),
        name="rwkv_prep_prompt",
    )(z3, z3, z0, *rw)


def _rwkv_prep_sample(z, zprev, rw):
    n = z.shape[0]
    w_ = RW_WIDTH
    return pl.pallas_call(
        _rwkv_prep_sample_kernel,
        grid=(1,),
        in_specs=[_const_spec((n, RW_COLS)), _const_spec((n, RW_COLS))] + [_const_spec(p.shape) for p in rw],
        out_specs=[_whole_spec((5, n, w_)), _whole_spec((n, w_)), _whole_spec((n, w_)), _whole_spec((n, w_))],
        out_shape=[jax.ShapeDtypeStruct((5, n, w_), F32)] + [jax.ShapeDtypeStruct((n, w_), F32)] * 3,
        compiler_params=_params("arbitrary"),
        name="rwkv_prep_sample",
    )(z, zprev, *rw)


SCAN_OPS = 5
SCAN_CHUNK = 32
N_BH = 16


def _tree_sum(parts):
    while len(parts) > 1:
        parts = [parts[i] + parts[i + 1] for i in range(0, len(parts), 2)]
    return parts[0]


def _rwkv_scan_kernel(opp_ref, oppn_ref, vv_ref, s0_ref, y_ref, sout_ref, s_ref, xr0_ref, xr1_ref):
    step = pl.program_id(0)
    tc = vv_ref.shape[0]
    nk = RW_HEAD_DIM

    def build_pair(src_ref, dst_ref, tp):
        for op in range(SCAN_OPS):
            rows = [jnp.broadcast_to(src_ref[op, bh, pl.ds(tp, 1), :], (8, LANES)) for bh in range(N_BH)]
            m_out = jnp.concatenate(rows, axis=0).T
            dst_ref[op, 2 * tp] = m_out[:nk]
            dst_ref[op, 2 * tp + 1] = m_out[nk:]

    @pl.when(step == 0)
    def _():
        s_ref[...] = s0_ref[...]

        def first(tp, carry):
            build_pair(opp_ref, xr0_ref, tp)
            return carry

        lax.fori_loop(0, tc // 2, first, 0)

    def run(cur_ref, nxt_ref):
        def token(t):
            def opnd(op, k):
                return jnp.broadcast_to(cur_ref[op, t, pl.ds(k, 1), :], (8, LANES))

            accs = [None] * 4
            for k in range(nk):
                p = s_ref[k] * opnd(0, k)
                accs[k % 4] = p if accs[k % 4] is None else accs[k % 4] + p
            sa = -_tree_sum(accs)
            vv = vv_ref[t]
            yacc = [None] * 4
            for k in range(nk):
                s_new = s_ref[k] * opnd(1, k) + sa * opnd(2, k) + vv * opnd(3, k)
                s_ref[k] = s_new
                p = s_new * opnd(4, k)
                yacc[k % 4] = p if yacc[k % 4] is None else yacc[k % 4] + p
            y_ref[t] = _tree_sum(yacc)

        def pair(tp, carry):
            build_pair(oppn_ref, nxt_ref, tp)
            token(2 * tp)
            token(2 * tp + 1)
            return carry

        lax.fori_loop(0, tc // 2, pair, 0)

    @pl.when(step % 2 == 0)
    def _():
        run(xr0_ref, xr1_ref)

    @pl.when(step % 2 == 1)
    def _():
        run(xr1_ref, xr0_ref)

    @pl.when(step == pl.num_programs(0) - 1)
    def _():
        sout_ref[...] = s_ref[...]


def _rwkv_scan(opp, vvp, s0p):
    t = vvp.shape[0]
    tc = SCAN_CHUNK
    nsteps = t // tc
    nk = RW_HEAD_DIM
    return pl.pallas_call(
        _rwkv_scan_kernel,
        grid=(nsteps,),
        in_specs=[pl.BlockSpec((SCAN_OPS, N_BH, tc // 2, LANES), lambda i: (0, 0, i, 0)),
                  pl.BlockSpec((SCAN_OPS, N_BH, tc // 2, LANES), lambda i: (0, 0, jnp.minimum(i + 1, nsteps - 1), 0)),
                  pl.BlockSpec((tc, 8, LANES), lambda i: (i, 0, 0)),
                  pl.BlockSpec((nk, 8, LANES), lambda i: (0, 0, 0))],
        out_specs=[pl.BlockSpec((tc, 8, LANES), lambda i: (i, 0, 0)),
                   pl.BlockSpec((nk, 8, LANES), lambda i: (0, 0, 0))],
        out_shape=[jax.ShapeDtypeStruct((t, 8, LANES), F32), jax.ShapeDtypeStruct((nk, 8, LANES), F32)],
        scratch_shapes=[pltpu.VMEM((nk, 8, LANES), F32),
                        pltpu.VMEM((SCAN_OPS, tc, nk, LANES), F32),
                        pltpu.VMEM((SCAN_OPS, tc, nk, LANES), F32)],
        compiler_params=_params("arbitrary"),
        name="rwkv_scan",
    )(opp, opp, vvp, s0p)


def _unpack_scan_state(sp, bsz):
    x = sp.reshape(RW_HEAD_DIM, 8, bsz, RW_HEADS, 8)
    return jnp.transpose(x, (2, 3, 1, 4, 0)).reshape(bsz, RW_HEADS, RW_HEAD_DIM, RW_HEAD_DIM)


def _rwkv_step_kernel(s_ref, ops_ref, v_ref, so_ref, y_ref):
    s = s_ref[...]
    kk, w, b, kh, r = (ops_ref[j] for j in range(SCAN_OPS))
    n = RW_HEAD_DIM
    eye = lax.broadcasted_iota(jnp.int32, (n, n), 0) == lax.broadcasted_iota(jnp.int32, (n, n), 1)
    sa = -jnp.sum(s * kk, axis=-1, keepdims=True)
    vcol = jnp.sum(jnp.where(eye, v_ref[...], 0.0), axis=-1, keepdims=True)
    s_new = s * w + sa * b + vcol * kh
    so_ref[...] = s_new
    ycol = jnp.sum(s_new * r, axis=-1, keepdims=True)
    y_ref[...] = jnp.sum(jnp.where(eye, ycol, 0.0), axis=1, keepdims=True)


def _rwkv_step(state, ops, v, tg):
    nbh, n, _ = state.shape
    return pl.pallas_call(
        _rwkv_step_kernel,
        grid=(nbh // tg,),
        in_specs=[pl.BlockSpec((tg, n, n), lambda i: (i, 0, 0)),
                  pl.BlockSpec((SCAN_OPS, tg, 1, n), lambda i: (0, i, 0, 0)),
                  pl.BlockSpec((tg, 1, n), lambda i: (i, 0, 0))],
        out_specs=[pl.BlockSpec((tg, n, n), lambda i: (i, 0, 0)),
                   pl.BlockSpec((tg, 1, n), lambda i: (i, 0, 0))],
        out_shape=[jax.ShapeDtypeStruct((nbh, n, n), F32), jax.ShapeDtypeStruct((nbh, 1, n), F32)],
        compiler_params=_params("arbitrary"),
        name="rwkv_step",
    )(state, ops, v)


def _norm_rope(x, gain, cosf, s1, s2, ones):
    outs = []
    for j in range(x.shape[1] // LANES):
        xj = x[:, j * LANES:(j + 1) * LANES]
        ms = _segsum(xj * xj, ones) * (1.0 / SWA_HEAD_DIM)
        xn = xj * lax.rsqrt(ms + NORM_EPS) * gain
        outs.append(xn * cosf + pltpu.roll(xn, LANES - ROPE_DIM // 2, axis=1) * s1
                    + pltpu.roll(xn, ROPE_DIM // 2, axis=1) * s2)
    return outs[0] if len(outs) == 1 else jnp.concatenate(outs, axis=1)


def _sink_softmax_pv(s, sink, v_bf16):
    m = jnp.maximum(jnp.max(s, axis=-1, keepdims=True), sink)
    p = jnp.exp(s - m)
    den = jnp.sum(p, axis=-1, keepdims=True) + jnp.exp(sink - m)
    return _dot(p.astype(BF16), v_bf16) / den


def _swa_prompt_kernel(sink_ref, q_ref, k_ref, v_ref, cos_ref, s1_ref, s2_ref, qg_ref, kg_ref, ones_ref,
                       o_ref, kkeep_ref, kprev_ref, vprev_ref):
    i = pl.program_id(1)
    blk = q_ref.shape[0]

    @pl.when(i == 0)
    def _():
        kprev_ref[...] = jnp.zeros_like(kprev_ref)
        vprev_ref[...] = jnp.zeros_like(vprev_ref)

    cosf, s1, s2, ones = cos_ref[...], s1_ref[...], s2_ref[...], ones_ref[...]
    qh = _norm_rope(q_ref[...], qg_ref[...], cosf, s1, s2, ones).astype(BF16)
    kc = _norm_rope(k_ref[...], kg_ref[...], cosf, s1, s2, ones)
    v = v_ref[...]
    kcat = jnp.concatenate([kprev_ref[...], kc], axis=0).astype(BF16)
    vcat = jnp.concatenate([vprev_ref[...], v], axis=0).astype(BF16)
    r = lax.broadcasted_iota(jnp.int32, (blk, 2 * blk), 0)
    c = lax.broadcasted_iota(jnp.int32, (blk, 2 * blk), 1)
    valid = (c >= r) & (c <= r + WINDOW) & ((c >= blk) | (i > 0))
    d = SWA_HEAD_DIM
    scores = []
    for h in range(SWA_HEADS):
        g = h // SWA_GROUPS
        s = lax.dot_general(qh[:, h * d:(h + 1) * d], kcat[:, g * d:(g + 1) * d], (((1,), (1,)), ((), ())),
                            preferred_element_type=F32) * (d ** -0.5)
        scores.append(jnp.where(valid, s, -jnp.inf))
    outs = [_sink_softmax_pv(scores[h], sink_ref[h], vcat[:, (h // SWA_GROUPS) * d:(h // SWA_GROUPS + 1) * d])
            for h in range(SWA_HEADS)]
    o_ref[...] = jnp.concatenate(outs, axis=1)
    kprev_ref[...] = kc
    vprev_ref[...] = v

    @pl.when(i == pl.num_programs(1) - 1)
    def _():
        kkeep_ref[...] = kc


def _swa_prompt(q3, k3, v3, tables, qg, kg, sinks, ones):
    bsz, t, _ = q3.shape
    blk = WINDOW
    row = lambda width: pl.BlockSpec((None, blk, width), lambda b, i: (b, i, 0))
    tab = pl.BlockSpec((blk, LANES), lambda b, i: (i, 0))
    return pl.pallas_call(
        _swa_prompt_kernel,
        grid=(bsz, t // blk),
        in_specs=[pl.BlockSpec(memory_space=pltpu.SMEM), row(SWA_Q), row(SWA_KV), row(SWA_KV), tab, tab, tab,
                  _const_spec((1, LANES)), _const_spec((1, LANES)), _const_spec((LANES, LANES))],
        out_specs=[row(SWA_Q), pl.BlockSpec((None, blk, SWA_KV), lambda b, i: (b, 0, 0))],
        out_shape=[jax.ShapeDtypeStruct((bsz, t, SWA_Q), F32), jax.ShapeDtypeStruct((bsz, blk, SWA_KV), F32)],
        scratch_shapes=[pltpu.VMEM((blk, SWA_KV), F32), pltpu.VMEM((blk, SWA_KV), F32)],
        compiler_params=_params("arbitrary", "arbitrary"),
        name="swa_prompt",
    )(sinks, q3, k3, v3, *tables, qg, kg, ones)


def _qk_rope_kernel(q_ref, k_ref, cos_ref, s1_ref, s2_ref, qg_ref, kg_ref, ones_ref, q8_ref, ko_ref):
    cosf, s1, s2, ones = cos_ref[...], s1_ref[...], s2_ref[...], ones_ref[...]
    qh = _norm_rope(q_ref[...], qg_ref[...], cosf, s1, s2, ones)
    lane_group = lax.broadcasted_iota(jnp.int32, (qh.shape[0], LANES), 1) // SWA_HEAD_DIM
    for h in range(SWA_HEADS):
        m, parity = divmod(h, 2)
        g = h // SWA_GROUPS
        piece = qh[:, m * LANES:(m + 1) * LANES]
        if parity != g:
            piece = pltpu.roll(piece, SWA_HEAD_DIM, axis=1)
        q8_ref[:, h, :] = jnp.where(lane_group == g, piece, 0.0)
    ko_ref[...] = _norm_rope(k_ref[...], kg_ref[...], cosf, s1, s2, ones)


def _qk_rope(q, k, tables, qg, kg, ones):
    n = q.shape[0]
    args = (q, k, *tables, qg, kg, ones)
    shapes = ((n, SWA_HEADS, SWA_KV), (n, SWA_KV))
    return pl.pallas_call(
        _qk_rope_kernel,
        grid=(1,),
        in_specs=[_const_spec(a.shape) for a in args],
        out_specs=[_whole_spec(s) for s in shapes],
        out_shape=[jax.ShapeDtypeStruct(s, F32) for s in shapes],
        compiler_params=_params("arbitrary"),
        name="qk_rope_sample",
    )(*args)


def _swa_sample_kernel(sink_ref, q8_ref, kn_ref, vn_ref, ck_ref, cv_ref, o_ref, cko_ref, cvo_ref):
    d = SWA_HEAD_DIM
    scale = d ** -0.5
    nrows = q8_ref.shape[0]
    head = lax.broadcasted_iota(jnp.int32, (SWA_HEADS, 1), 0)
    lane_group = lax.broadcasted_iota(jnp.int32, (1, SWA_KV), 1) // d
    sink = jnp.zeros((SWA_HEADS, 1), F32)
    for h in range(SWA_HEADS):
        sink = jnp.where(head == h, sink_ref[h], sink)
    nt = (((1,), (1,)), ((), ()))
    q8 = [q8_ref[b] for b in range(nrows)]
    kn = [kn_ref[pl.ds(b, 1), :] for b in range(nrows)]
    vn = [vn_ref[pl.ds(b, 1), :] for b in range(nrows)]
    s_past = [lax.dot_general(q8[b].astype(BF16), ck_ref[b].astype(BF16), nt, preferred_element_type=F32) * scale
              for b in range(nrows)]
    s_new = [jnp.sum(q8[b] * kn[b], axis=-1, keepdims=True) * scale for b in range(nrows)]
    probs = []
    for b in range(nrows):
        mx = jnp.maximum(jnp.maximum(jnp.max(s_past[b], axis=-1, keepdims=True), s_new[b]), sink)
        p_past, p_new = jnp.exp(s_past[b] - mx), jnp.exp(s_new[b] - mx)
        den = jnp.sum(p_past, axis=-1, keepdims=True) + p_new + jnp.exp(sink - mx)
        probs.append((p_past, p_new, den))
    for b in range(nrows):
        p_past, p_new, den = probs[b]
        o = (_dot(p_past.astype(BF16), cv_ref[b].astype(BF16)) + p_new * vn[b]) / den
        pairs = []
        for m in range(SWA_HEADS // 2):
            g = (2 * m) // SWA_GROUPS
            first, second = o[2 * m:2 * m + 1, :], o[2 * m + 1:2 * m + 2, :]
            if g == 0:
                second = pltpu.roll(second, d, axis=1)
            else:
                first = pltpu.roll(first, d, axis=1)
            pairs.append(jnp.where(lane_group == 0, first, second))
        o_ref[pl.ds(b, 1), :] = jnp.concatenate(pairs, axis=1)
        cko_ref[b] = jnp.concatenate([ck_ref[b][1:], kn[b]], axis=0)
        cvo_ref[b] = jnp.concatenate([cv_ref[b][1:], vn[b]], axis=0)


def _swa_sample(q8, kh, v, cache_k, cache_v, sinks, tb):
    n = q8.shape[0]
    w = cache_k.shape[1]
    cache = pl.BlockSpec((tb, w, SWA_KV), lambda i: (i, 0, 0))
    row = lambda width: pl.BlockSpec((tb, width), lambda i: (i, 0))
    return pl.pallas_call(
        _swa_sample_kernel,
        grid=(n // tb,),
        in_specs=[pl.BlockSpec(memory_space=pltpu.SMEM),
                  pl.BlockSpec((tb, SWA_HEADS, SWA_KV), lambda i: (i, 0, 0)), row(SWA_KV), row(SWA_KV),
                  cache, cache],
        out_specs=[row(SWA_Q), cache, cache],
        out_shape=[jax.ShapeDtypeStruct((n, SWA_Q), F32),
                   jax.ShapeDtypeStruct(cache_k.shape, F32), jax.ShapeDtypeStruct(cache_v.shape, F32)],
        compiler_params=_params("arbitrary"),
        name="swa_sample",
    )(sinks, q8, kh, v, cache_k, cache_v)


def _mem_kv_kernel(mem_ref, g_ref, w_ref, kg_ref, mk_ref, mv_ref):
    kv = _dot(_rms(mem_ref[...], g_ref[...]).astype(BF16), w_ref[...])
    kg = kg_ref[...]
    mk_ref[...] = jnp.concatenate(
        [_rms(kv[:, h * MEM_HEAD_DIM:(h + 1) * MEM_HEAD_DIM], kg) for h in range(MEM_HEADS)], axis=1)
    mv_ref[...] = kv[:, MEM_WIDTH:]


def _mem_kv(mem, gain, w_bf16, kgain):
    bsz, m, _ = mem.shape
    out = pl.BlockSpec((None, m, MEM_WIDTH), lambda b: (b, 0, 0))
    return pl.pallas_call(
        _mem_kv_kernel,
        grid=(bsz,),
        in_specs=[pl.BlockSpec((None, m, D_MODEL), lambda b: (b, 0, 0)), _const_spec((1, D_MODEL)),
                  _const_spec((D_MODEL, 2 * MEM_WIDTH)), _const_spec((1, MEM_HEAD_DIM))],
        out_specs=[out, out],
        out_shape=[jax.ShapeDtypeStruct((bsz, m, MEM_WIDTH), F32)] * 2,
        compiler_params=_params("arbitrary"),
        name="mem_kv",
    )(mem, gain, w_bf16, kgain)


def _mem_attn_prompt_kernel(xq_ref, mk_ref, mv_ref, qg_ref, o_ref):
    xq = xq_ref[...]
    qg = qg_ref[...]
    d = MEM_HEAD_DIM
    scores = []
    for h in range(MEM_HEADS):
        lanes = slice(h * d, (h + 1) * d)
        q = _rms(xq[:, lanes], qg).astype(BF16)
        scores.append(lax.dot_general(q, mk_ref[:, lanes].astype(BF16), (((1,), (1,)), ((), ())),
                                      preferred_element_type=F32) * (d ** -0.5))
    outs = []
    for h in range(MEM_HEADS):
        s = scores[h]
        p = jnp.exp(s - jnp.max(s, axis=-1, keepdims=True))
        den = jnp.sum(p, axis=-1, keepdims=True)
        outs.append(_dot(p.astype(BF16), mv_ref[:, h * d:(h + 1) * d].astype(BF16)) / den)
    o_ref[...] = jnp.concatenate(outs, axis=1)


def _mem_attn_prompt(xq3, mk, mv, qgain, tq):
    bsz, t, _ = xq3.shape
    m = mk.shape[1]
    mem = pl.BlockSpec((None, m, MEM_WIDTH), lambda b, i: (b, 0, 0))
    row = pl.BlockSpec((None, tq, MEM_WIDTH), lambda b, i: (b, i, 0))
    return pl.pallas_call(
        _mem_attn_prompt_kernel,
        grid=(bsz, t // tq),
        in_specs=[row, mem, mem, _const_spec((1, MEM_HEAD_DIM))],
        out_specs=row,
        out_shape=jax.ShapeDtypeStruct((bsz, t, MEM_WIDTH), F32),
        compiler_params=_params("arbitrary", "arbitrary"),
        name="mem_attn_prompt",
    )(xq3, mk, mv, qgain)


def _mem_attn_sample_kernel(xq_ref, mk_ref, mv_ref, qg_ref, o_ref):
    d = MEM_HEAD_DIM
    qg = qg_ref[...]
    for b in range(xq_ref.shape[0]):
        q = _rms(xq_ref[b], qg)
        s = jnp.sum(mk_ref[b] * q, axis=-1, keepdims=True) * (d ** -0.5)
        p = jnp.exp(s - jnp.max(s, axis=0, keepdims=True))
        o_ref[b] = jnp.sum(p * mv_ref[b], axis=0) / jnp.sum(p, axis=0)


def _mem_attn_sample(xq3, mk5, mv5, qgain, tb):
    _, n, m, _, _ = mk5.shape
    mem = pl.BlockSpec((None, tb, m, MEM_HEADS, MEM_HEAD_DIM), lambda i: (0, i, 0, 0, 0))
    row = pl.BlockSpec((tb, MEM_HEADS, MEM_HEAD_DIM), lambda i: (i, 0, 0))
    return pl.pallas_call(
        _mem_attn_sample_kernel,
        grid=(n // tb,),
        in_specs=[row, mem, mem, _const_spec((1, MEM_HEAD_DIM))],
        out_specs=row,
        out_shape=jax.ShapeDtypeStruct((n, MEM_HEADS, MEM_HEAD_DIM), F32),
        compiler_params=_params("arbitrary"),
        name="mem_attn_sample",
    )(xq3, mk5, mv5, qgain)


def _values_from_tiles(y_ref):
    tm = y_ref.shape[0]
    left = lax.broadcasted_iota(jnp.int32, (tm, LANES), 1) < RW_HEAD_DIM
    groups = [[], []]
    for m in range(RW_WIDTH // LANES):
        even, odd = y_ref[:, 2 * m, :], y_ref[:, 2 * m + 1, :]
        groups[0].append(jnp.where(left, even, pltpu.roll(odd, RW_HEAD_DIM, axis=1)))
        groups[1].append(jnp.where(left, pltpu.roll(even, RW_HEAD_DIM, axis=1), odd))
    return jnp.concatenate([jnp.concatenate(g, axis=1) for g in groups], axis=0)


def _merge_ffn_kernel(x_ref, y_ref, bonus_ref, grw_ref, ob_ref, oc_ref, nmix_ref, wg_ref, lnw_ref, lnb_ref,
                      ones_ref, wa_ref, wb_ref, wc_ref, wo_ref, nffn_ref, wu_ref, wd_ref, out_ref, *, y_tiles):
    nbatch, tm, _ = x_ref.shape
    rows = lambda ref: ref[...].reshape(nbatch * tm, ref.shape[-1])
    x = rows(x_ref)
    gates = jax.nn.sigmoid(_dot(_rms(x, nmix_ref[...]).astype(BF16), wg_ref[...]))
    y = _values_from_tiles(y_ref) if y_tiles else rows(y_ref)
    ones = ones_ref[...]
    inv_n = 1.0 / RW_HEAD_DIM
    yc = y - _segsum(y, ones) * inv_n
    var = _segsum(yc * yc, ones) * inv_n
    out_a = (yc * lax.rsqrt(var + GN_EPS) * lnw_ref[...] + lnb_ref[...] + rows(bonus_ref)) * rows(grw_ref)
    merged = (gates[:, :D_MODEL] * _dot(out_a.astype(BF16), wa_ref[...])
              + gates[:, D_MODEL:2 * D_MODEL] * _dot(rows(ob_ref).astype(BF16), wb_ref[...])
              + gates[:, 2 * D_MODEL:] * _dot(rows(oc_ref).astype(BF16), wc_ref[...]))
    x1 = x + _dot(merged.astype(BF16), wo_ref[...])
    up = _dot(_rms(x1, nffn_ref[...]).astype(BF16), wu_ref[...])
    act = jnp.square(jnp.maximum(up, 0.0))
    out_ref[...] = (x1 + _dot(act.astype(BF16), wd_ref[...])).reshape(nbatch, tm, D_MODEL)


def _merge_ffn(x, y, bonus, grw, ob, oc, weights, tm):
    nbatch, t, _ = x.shape
    y_tiles = y.ndim == 3 and y.shape[1:] == (8, LANES) and nbatch == 2
    row = lambda width: pl.BlockSpec((nbatch, tm, width), lambda i: (0, i, 0))
    y_spec = pl.BlockSpec((tm, 8, LANES), lambda i: (i, 0, 0)) if y_tiles else row(RW_WIDTH)
    return pl.pallas_call(
        functools.partial(_merge_ffn_kernel, y_tiles=y_tiles),
        grid=(t // tm,),
        in_specs=[row(D_MODEL), y_spec] + [row(RW_WIDTH)] * 4 + [_const_spec(w.shape) for w in weights],
        out_specs=row(D_MODEL),
        out_shape=jax.ShapeDtypeStruct((nbatch, t, D_MODEL), F32),
        compiler_params=_params("arbitrary"),
        name="merge_ffn",
    )(x, y, bonus, grw, ob, oc, *weights)


def _rope_tables(pos):
    half = ROPE_DIM // 2
    inv_freq = jnp.power(jnp.float32(ROPE_THETA), -jnp.arange(half, dtype=F32) * (2.0 / ROPE_DIM))
    ang = pos.astype(F32)[:, None] * inv_freq[None, :]
    cos, sin = jnp.cos(ang), jnp.sin(ang)
    n = pos.shape[0]
    rest = SWA_HEAD_DIM - ROPE_DIM
    z8, zr = jnp.zeros((n, half), F32), jnp.zeros((n, rest), F32)
    cosf = jnp.concatenate([cos, cos, jnp.ones((n, rest), F32)], axis=1)
    s1 = jnp.concatenate([-sin, z8, zr], axis=1)
    s2 = jnp.concatenate([z8, sin, zr], axis=1)
    return tuple(jnp.tile(tbl, (1, LANES // SWA_HEAD_DIM)) for tbl in (cosf, s1, s2))


def kernel(x_prompt, x_sample, state_rwkv, state_rwkv_shift, cache_swa_k, cache_swa_v, cache_mem_k, cache_mem_v, mem_prompt, norm_mix, w_in, rw_mu, rw_w0, rw_w2, rw_a0, rw_a2, rw_g2, rw_k_k, rw_k_a, rw_r_k, rw_ln_w, rw_ln_b, q_norm, k_norm, swa_sinks, mem_norm, w_mem_kv, xq_norm, xk_norm, w_br_a, w_br_b, w_br_c, w_out, norm_ffn, w_up, w_down):
    bsz, t, _ = x_prompt.shape
    nb = x_sample.shape[0]
    assert w_in.shape[0] == 1 and x_sample.shape[1] == 1

    row = lambda p: p.reshape(1, -1)
    w_ = RW_WIDTH
    vcols = slice(2 * w_, 3 * w_)

    def tile_order(a, axis=-1):
        a = jnp.moveaxis(a, axis, -1)
        a = a.reshape(a.shape[:-1] + (RW_HEADS, 8, 8)).swapaxes(-3, -2).reshape(a.shape)
        return jnp.moveaxis(a, -1, axis)

    def tile_order_vcols(a):
        return jnp.concatenate([a[..., :2 * w_], tile_order(a[..., vcols]), a[..., 3 * w_:]], axis=-1)

    w_proj = tile_order_vcols(w_in[0][:, :PROJ_COLS]).astype(BF16)
    w_gate = w_in[0][:, PROJ_COLS:].astype(BF16)
    zeros_lora = jnp.zeros((64, w_), F32)
    w2pad = jnp.concatenate([rw_w2[0], zeros_lora], axis=0)
    a2pad = jnp.concatenate([zeros_lora, rw_a2[0]], axis=0)
    hi_lo = lambda w: (w.astype(BF16), (w - w.astype(BF16).astype(F32)).astype(BF16))
    head_nat = jnp.arange(w_) // RW_HEAD_DIM
    head_tile = (jnp.arange(w_) // 8) % RW_HEADS
    ones_kk = (head_nat[:, None] == head_nat[None, :]).astype(BF16)
    ones_kv = (head_nat[:, None] == head_tile[None, :]).astype(BF16)
    ones_vv = (head_tile[:, None] == head_tile[None, :]).astype(BF16)
    rw = (row(tile_order_vcols(rw_mu[0])), row(rw_w0[0]), *hi_lo(w2pad), row(rw_a0[0]), *hi_lo(a2pad),
          *hi_lo(tile_order(rw_g2[0])), row(rw_k_k[0]), row(rw_k_a[0]), row(rw_r_k[0]), ones_kk, ones_kv)
    qg = jnp.tile(row(q_norm[0]), (1, LANES // SWA_HEAD_DIM))
    kg = jnp.tile(row(k_norm[0]), (1, LANES // SWA_HEAD_DIM))
    ones_swa = _block_ones(LANES, SWA_HEAD_DIM)
    sinks = swa_sinks[0]
    merge_w = (row(norm_mix[0]), w_gate, row(tile_order(rw_ln_w[0])), row(tile_order(rw_ln_b[0])), ones_vv,
               tile_order(w_br_a[0], axis=0).astype(BF16), w_br_b[0].astype(BF16), w_br_c[0].astype(BF16),
               w_out[0].astype(BF16), row(norm_ffn[0]), w_up[0].astype(BF16), w_down[0].astype(BF16))

    xp = x_prompt.reshape(bsz * t, D_MODEL)
    z, q, k, v, xq = _proj(xp, row(norm_mix[0]), w_proj, 512)
    z3 = z.reshape(bsz, t, RW_COLS)
    ops, vv, grw, bonus = _rwkv_prep_prompt(z3, jnp.zeros((bsz, 1, RW_COLS), F32), rw, 256)
    yp, sp = _rwkv_scan(ops, vv, jnp.zeros((RW_HEAD_DIM, 8, LANES), F32))
    state_p = _unpack_scan_state(sp, bsz)

    tables_p = _rope_tables(jnp.arange(t, dtype=jnp.int32))
    k3, v3 = k.reshape(bsz, t, SWA_KV), v.reshape(bsz, t, SWA_KV)
    out_b, k_keep = _swa_prompt(q.reshape(bsz, t, SWA_Q), k3, v3, tables_p, qg, kg, sinks, ones_swa)
    v_keep = v3[:, t - WINDOW:]

    mk, mv = _mem_kv(mem_prompt, row(mem_norm[0]), w_mem_kv[0].astype(BF16), row(xk_norm[0]))
    out_c = _mem_attn_prompt(xq.reshape(bsz, t, MEM_WIDTH), mk, mv, row(xq_norm[0]), 256)

    y_prompt = _merge_ffn(x_prompt, yp, bonus, grw, out_b, out_c, merge_w, 128)

    xs = x_sample.reshape(nb, D_MODEL)
    zs, qs, ks, vs, xqs = _proj(xs, row(norm_mix[0]), w_proj, nb)
    ops_s, vrw_s, grw_s, bonus_s = _rwkv_prep_sample(zs, tile_order_vcols(state_rwkv_shift[0]), rw)
    nbh = nb * RW_HEADS
    state_s, y_s = _rwkv_step(state_rwkv[0].reshape(nbh, RW_HEAD_DIM, RW_HEAD_DIM),
                              ops_s.reshape(SCAN_OPS, nbh, 1, RW_HEAD_DIM),
                              tile_order(vrw_s).reshape(nbh, 1, RW_HEAD_DIM), 32)

    past = cache_swa_k.shape[2]
    assert past <= WINDOW and past <= PAST_LEN
    tables_s = _rope_tables(jnp.full((1,), PAST_LEN, dtype=jnp.int32))
    q8_s, kh_s = _qk_rope(qs, ks, tables_s, qg, kg, ones_swa)
    ob_s, ck_new, cv_new = _swa_sample(q8_s, kh_s, vs, cache_swa_k[0].reshape(nb, past, SWA_KV),
                                       cache_swa_v[0].reshape(nb, past, SWA_KV), sinks, 8)
    oc_s = _mem_attn_sample(xqs.reshape(nb, MEM_HEADS, MEM_HEAD_DIM), cache_mem_k, cache_mem_v,
                            row(xq_norm[0]), 8)
    y_sample = _merge_ffn(xs[None], tile_order(y_s.reshape(nb, w_))[None], bonus_s[None], grw_s[None],
                          ob_s.reshape(1, nb, SWA_Q), oc_s.reshape(1, nb, MEM_WIDTH), merge_w, nb)

    kv5 = lambda a, n_: a.reshape(1, n_, -1, SWA_KV_HEADS, SWA_HEAD_DIM)
    mem5 = lambda a: a.reshape(1, bsz, -1, MEM_HEADS, MEM_HEAD_DIM)
    return (y_prompt,
            y_sample.reshape(nb, 1, D_MODEL),
            state_p[None],
            tile_order_vcols(z3[:, t - 1])[None],
            kv5(k_keep, bsz), kv5(v_keep, bsz),
            mem5(mk), mem5(mv),
            state_s.reshape(1, nb, RW_HEADS, RW_HEAD_DIM, RW_HEAD_DIM),
            tile_order_vcols(zs)[None],
            kv5(ck_new, nb), kv5(cv_new, nb))
```

```python
import functools

import jax
import jax.numpy as jnp
from jax import lax
from jax.experimental import pallas as pl
from jax.experimental.pallas import tpu as pltpu

F32 = jnp.float32
BF16 = jnp.bfloat16

D_MODEL = 1024
RW_HEADS = 8
RW_HEAD_DIM = 64
RW_WIDTH = RW_HEADS * RW_HEAD_DIM
RW_COLS = 3 * RW_WIDTH + 64 + 64 + 128
GN_EPS = 64e-5
SWA_HEADS = 8
SWA_KV_HEADS = 2
SWA_GROUPS = SWA_HEADS // SWA_KV_HEADS
SWA_HEAD_DIM = 64
SWA_Q = SWA_HEADS * SWA_HEAD_DIM
SWA_KV = SWA_KV_HEADS * SWA_HEAD_DIM
WINDOW = 128
PAST_LEN = 8192
ROPE_THETA = 500000.0
ROPE_DIM = SWA_HEAD_DIM // 4
MEM_HEADS = 4
MEM_HEAD_DIM = 128
MEM_WIDTH = MEM_HEADS * MEM_HEAD_DIM
D_FF = 4 * D_MODEL
NORM_EPS = 1e-5
PROJ_COLS = RW_COLS + SWA_Q + 2 * SWA_KV + MEM_WIDTH

LANES = 128
VMEM_LIMIT = 52 * 1024 * 1024


def _params(*sem):
    return pltpu.CompilerParams(dimension_semantics=sem, vmem_limit_bytes=VMEM_LIMIT)


def _const_spec(shape):
    nd = len(shape)
    return pl.BlockSpec(shape, lambda *_: (0,) * nd, pipeline_mode=pl.Buffered(1))


def _whole_spec(shape):
    nd = len(shape)
    return pl.BlockSpec(shape, lambda *_: (0,) * nd)


def _rms(x, gain):
    return x * lax.rsqrt(jnp.mean(x * x, axis=-1, keepdims=True) + NORM_EPS) * gain


def _split(x):
    hi = x.astype(BF16)
    return hi, (x - hi.astype(F32)).astype(BF16)


def _dot(a, b):
    return jnp.dot(a, b, preferred_element_type=F32)


def _segsum(x, ones):
    hi, lo = _split(x)
    return _dot(hi, ones) + _dot(lo, ones)


def _dot3(x, w_hi, w_lo):
    hi, lo = _split(x)
    return _dot(hi, w_hi) + _dot(lo, w_hi) + _dot(hi, w_lo)


def _block_ones(n, seg):
    idx = jnp.arange(n) // seg
    return (idx[:, None] == idx[None, :]).astype(BF16)


def _proj_kernel(x_ref, g_ref, w_ref, z_ref, q_ref, k_ref, v_ref, xq_ref):
    h = _rms(x_ref[...], g_ref[...])
    p = _dot(h.astype(BF16), w_ref[...])
    c0, c1, c2, c3 = RW_COLS, RW_COLS + SWA_Q, RW_COLS + SWA_Q + SWA_KV, RW_COLS + SWA_Q + 2 * SWA_KV
    z_ref[...] = p[:, :c0]
    q_ref[...] = p[:, c0:c1]
    k_ref[...] = p[:, c1:c2]
    v_ref[...] = p[:, c2:c3]
    xq_ref[...] = p[:, c3:]


def _proj(x, gain, w_bf16, tm):
    n = x.shape[0]
    widths = (RW_COLS, SWA_Q, SWA_KV, SWA_KV, MEM_WIDTH)
    return pl.pallas_call(
        _proj_kernel,
        grid=(n // tm,),
        in_specs=[pl.BlockSpec((tm, D_MODEL), lambda i: (i, 0)),
                  _const_spec((1, D_MODEL)),
                  _const_spec((D_MODEL, PROJ_COLS))],
        out_specs=[pl.BlockSpec((tm, w), lambda i: (i, 0)) for w in widths],
        out_shape=[jax.ShapeDtypeStruct((n, w), F32) for w in widths],
        compiler_params=_params("arbitrary"),
        name="in_proj",
    )(x, gain, w_bf16)


def _rwkv_prep_math(z, zprev, mu, w0, w2h, w2l, a0, a2h, a2l, g2h, g2l, k_k, k_a, r_k, ones, ones_v):
    zs = z + (zprev - z) * mu
    w_ = RW_WIDTH
    r, k, v = zs[:, :w_], zs[:, w_:2 * w_], zs[:, 2 * w_:3 * w_]
    u = zs[:, 3 * w_:3 * w_ + LANES]
    gd = zs[:, 3 * w_ + LANES:]
    w_log = -jax.nn.softplus(-(w0 + _dot3(jnp.tanh(u), w2h, w2l))) - 0.5
    decay = jnp.exp(-jnp.exp(w_log))
    a = jax.nn.sigmoid(a0 + _dot3(u, a2h, a2l))
    g = _dot3(jax.nn.sigmoid(gd), g2h, g2l)
    kkr = k * k_k
    kk = kkr / jnp.maximum(jnp.sqrt(_segsum(kkr * kkr, ones)), 1e-12)
    kh = k * (1.0 + (a - 1.0) * k_a)
    bonus = _segsum(r * kh * r_k, ones_v) * v
    return kk, decay, kk * a, kh, r, v, g, bonus


def _rwkv_prep_prompt_kernel(z_ref, zb_ref, z0_ref, mu, w0, w2h, w2l, a0, a2h, a2l, g2h, g2l, k_k, k_a, r_k,
                             ones, ones_v, ops_ref, vv_ref, g_ref, bonus_ref, nat_ref):
    nbatch, tm, _ = z_ref.shape
    half = tm // 2
    left_half = lax.broadcasted_iota(jnp.int32, (half, LANES), 1) < RW_HEAD_DIM
    left_full = lax.broadcasted_iota(jnp.int32, (tm, LANES), 1) < RW_HEAD_DIM
    row = lax.broadcasted_iota(jnp.int32, (tm, RW_COLS), 0)
    values = []
    for b in range(nbatch):
        z = z_ref[b]
        first = jnp.where(pl.program_id(0) == 0, z0_ref[b], zb_ref[b, 7:8, :])
        zprev = jnp.where(row == 0, first, pltpu.roll(z, 1, axis=0))
        outs = _rwkv_prep_math(z, zprev, mu[...], w0[...], w2h[...], w2l[...], a0[...], a2h[...], a2l[...],
                               g2h[...], g2l[...], k_k[...], k_a[...], r_k[...], ones[...], ones_v[...])
        for j in range(SCAN_OPS):
            for m in range(RW_WIDTH // LANES):
                nat_ref[m] = outs[j][:, m * LANES:(m + 1) * LANES]
                ge = nat_ref[m, pl.ds(0, half, stride=2), :]
                go = nat_ref[m, pl.ds(1, half, stride=2), :]
                head = b * RW_HEADS + 2 * m
                ops_ref[j, head] = jnp.where(left_half, ge, pltpu.roll(go, RW_HEAD_DIM, axis=1))
                ops_ref[j, head + 1] = jnp.where(left_half, pltpu.roll(ge, RW_HEAD_DIM, axis=1), go)
        values.append(outs[5])
        g_ref[b] = outs[6]
        bonus_ref[b] = outs[7]
    assert nbatch == 2
    for vh in range(8):
        m, odd = divmod(vh, 2)
        g0, g1 = (val[:, m * LANES:(m + 1) * LANES] for val in values)
        if odd:
            vv_ref[:, vh, :] = jnp.where(left_full, pltpu.roll(g0, RW_HEAD_DIM, axis=1), g1)
        else:
            vv_ref[:, vh, :] = jnp.where(left_full, g0, pltpu.roll(g1, RW_HEAD_DIM, axis=1))


def _rwkv_prep_sample_kernel(z_ref, zp_ref, mu, w0, w2h, w2l, a0, a2h, a2l, g2h, g2l, k_k, k_a, r_k,
                             ones, ones_v, ops_ref, v_ref, g_ref, bonus_ref):
    outs = _rwkv_prep_math(z_ref[...], zp_ref[...], mu[...], w0[...], w2h[...], w2l[...], a0[...], a2h[...],
                           a2l[...], g2h[...], g2l[...], k_k[...], k_a[...], r_k[...], ones[...], ones_v[...])
    for j in range(5):
        ops_ref[j] = outs[j]
    v_ref[...] = outs[5]
    g_ref[...] = outs[6]
    bonus_ref[...] = outs[7]


def _rwkv_prep_prompt(z3, z0, rw, tm):
    bsz, t, _ = z3.shape
    w_ = RW_WIDTH
    row_spec = lambda width: pl.BlockSpec((bsz, tm, width), lambda i: (0, i, 0))
    return pl.pallas_call(
        _rwkv_prep_prompt_kernel,
        grid=(t // tm,),
        in_specs=[row_spec(RW_COLS),
                  pl.BlockSpec((bsz, 8, RW_COLS), lambda i: (0, jnp.maximum(i * (tm // 8) - 1, 0), 0)),
                  pl.BlockSpec((bsz, 1, RW_COLS), lambda i: (0, 0, 0))]
                 + [_const_spec(p.shape) for p in rw],
        out_specs=[pl.BlockSpec((SCAN_OPS, bsz * RW_HEADS, tm // 2, LANES), lambda i: (0, 0, i, 0)),
                   pl.BlockSpec((tm, 8, LANES), lambda i: (i, 0, 0)),
                   row_spec(w_), row_spec(w_)],
        out_shape=[jax.ShapeDtypeStruct((SCAN_OPS, bsz * RW_HEADS, t // 2, LANES), F32),
                   jax.ShapeDtypeStruct((t, 8, LANES), F32),
                   jax.ShapeDtypeStruct((bsz, t, w_), F32), jax.ShapeDtypeStruct((bsz, t, w_), F32)],
        scratch_shapes=[pltpu.VMEM((w_ // LANES, tm, LANES), F32)],
        compiler_params=_params("arbitrary"),
        name="rwkv_prep_prompt",
    )(z3, z3, z0, *rw)


def _rwkv_prep_sample(z, zprev, rw):
    n = z.shape[0]
    w_ = RW_WIDTH
    return pl.pallas_call(
        _rwkv_prep_sample_kernel,
        grid=(1,),
        in_specs=[_const_spec((n, RW_COLS)), _const_spec((n, RW_COLS))] + [_const_spec(p.shape) for p in rw],
        out_specs=[_whole_spec((5, n, w_)), _whole_spec((n, w_)), _whole_spec((n, w_)), _whole_spec((n, w_))],
        out_shape=[jax.ShapeDtypeStruct((5, n, w_), F32)] + [jax.ShapeDtypeStruct((n, w_), F32)] * 3,
        compiler_params=_params("arbitrary"),
        name="rwkv_prep_sample",
    )(z, zprev, *rw)


SCAN_OPS = 5
SCAN_CHUNK = 64
N_BH = 16


def _tree_sum(parts):
    while len(parts) > 1:
        parts = [parts[i] + parts[i + 1] for i in range(0, len(parts), 2)]
    return parts[0]


def _rwkv_scan_kernel(opp_ref, oppn_ref, vv_ref, s0_ref, y_ref, sout_ref, s_ref, xr0_ref, xr1_ref):
    step = pl.program_id(0)
    tc = vv_ref.shape[0]
    nk = RW_HEAD_DIM

    def build_pair(src_ref, dst_ref, tp):
        for op in range(SCAN_OPS):
            rows = [jnp.broadcast_to(src_ref[op, bh, pl.ds(tp, 1), :], (8, LANES)) for bh in range(N_BH)]
            m_out = jnp.concatenate(rows, axis=0).T
            dst_ref[op, 2 * tp] = m_out[:nk]
            dst_ref[op, 2 * tp + 1] = m_out[nk:]

    @pl.when(step == 0)
    def _():
        s_ref[...] = s0_ref[...]

        def first(tp, carry):
            build_pair(opp_ref, xr0_ref, tp)
            return carry

        lax.fori_loop(0, tc // 2, first, 0)

    def run(cur_ref, nxt_ref):
        def token(t):
            def opnd(op, k):
                return jnp.broadcast_to(cur_ref[op, t, pl.ds(k, 1), :], (8, LANES))

            accs = [None] * 4
            for k in range(nk):
                p = s_ref[k] * opnd(0, k)
                accs[k % 4] = p if accs[k % 4] is None else accs[k % 4] + p
            sa = -_tree_sum(accs)
            vv = vv_ref[t]
            yacc = [None] * 4
            for k in range(nk):
                s_new = s_ref[k] * opnd(1, k) + sa * opnd(2, k) + vv * opnd(3, k)
                s_ref[k] = s_new
                p = s_new * opnd(4, k)
                yacc[k % 4] = p if yacc[k % 4] is None else yacc[k % 4] + p
            y_ref[t] = _tree_sum(yacc)

        def pair(tp, carry):
            build_pair(oppn_ref, nxt_ref, tp)
            token(2 * tp)
            token(2 * tp + 1)
            return carry

        lax.fori_loop(0, tc // 2, pair, 0)

    @pl.when(step % 2 == 0)
    def _():
        run(xr0_ref, xr1_ref)

    @pl.when(step % 2 == 1)
    def _():
        run(xr1_ref, xr0_ref)

    @pl.when(step == pl.num_programs(0) - 1)
    def _():
        sout_ref[...] = s_ref[...]


def _rwkv_scan(opp, vvp, s0p):
    t = vvp.shape[0]
    tc = SCAN_CHUNK
    nsteps = t // tc
    nk = RW_HEAD_DIM
    return pl.pallas_call(
        _rwkv_scan_kernel,
        grid=(nsteps,),
        in_specs=[pl.BlockSpec((SCAN_OPS, N_BH, tc // 2, LANES), lambda i: (0, 0, i, 0)),
                  pl.BlockSpec((SCAN_OPS, N_BH, tc // 2, LANES), lambda i: (0, 0, jnp.minimum(i + 1, nsteps - 1), 0)),
                  pl.BlockSpec((tc, 8, LANES), lambda i: (i, 0, 0)),
                  pl.BlockSpec((nk, 8, LANES), lambda i: (0, 0, 0))],
        out_specs=[pl.BlockSpec((tc, 8, LANES), lambda i: (i, 0, 0)),
                   pl.BlockSpec((nk, 8, LANES), lambda i: (0, 0, 0))],
        out_shape=[jax.ShapeDtypeStruct((t, 8, LANES), F32), jax.ShapeDtypeStruct((nk, 8, LANES), F32)],
        scratch_shapes=[pltpu.VMEM((nk, 8, LANES), F32),
                        pltpu.VMEM((SCAN_OPS, tc, nk, LANES), F32),
                        pltpu.VMEM((SCAN_OPS, tc, nk, LANES), F32)],
        compiler_params=_params("arbitrary"),
        name="rwkv_scan",
    )(opp, opp, vvp, s0p)


def _unpack_scan_state(sp, bsz):
    x = sp.reshape(RW_HEAD_DIM, 8, bsz, RW_HEADS, 8)
    return jnp.transpose(x, (2, 3, 1, 4, 0)).reshape(bsz, RW_HEADS, RW_HEAD_DIM, RW_HEAD_DIM)


def _rwkv_step_kernel(s_ref, ops_ref, v_ref, so_ref, y_ref):
    kk, w, b, kh, r = (ops_ref[j] for j in range(SCAN_OPS))
    for vi in range(s_ref.shape[0]):
        s = s_ref[vi]
        sa = -jnp.sum(s * kk, axis=0, keepdims=True)
        s_new = s * w + sa * b + v_ref[pl.ds(vi, 1), :] * kh
        so_ref[vi] = s_new
        y_ref[pl.ds(vi, 1), :] = jnp.sum(s_new * r, axis=0, keepdims=True)


def _rwkv_step(state_t, ops_t, v_t):
    nh, n, _, nbatch = state_t.shape
    vec = pl.BlockSpec((n, nbatch), lambda h: (h, 0))
    mat = pl.BlockSpec((None, n, n, nbatch), lambda h: (h, 0, 0, 0))
    return pl.pallas_call(
        _rwkv_step_kernel,
        grid=(nh,),
        in_specs=[mat, pl.BlockSpec((SCAN_OPS, n, nbatch), lambda h: (0, h, 0)), vec],
        out_specs=[mat, vec],
        out_shape=[jax.ShapeDtypeStruct(state_t.shape, F32), jax.ShapeDtypeStruct((nh * n, nbatch), F32)],
        compiler_params=_params("arbitrary"),
        name="rwkv_step",
    )(state_t, ops_t, v_t)


def _norm_rope(x, gain, cosf, s1, s2, ones):
    outs = []
    for j in range(x.shape[1] // LANES):
        xj = x[:, j * LANES:(j + 1) * LANES]
        ms = _segsum(xj * xj, ones) * (1.0 / SWA_HEAD_DIM)
        xn = xj * lax.rsqrt(ms + NORM_EPS) * gain
        outs.append(xn * cosf + pltpu.roll(xn, LANES - ROPE_DIM // 2, axis=1) * s1
                    + pltpu.roll(xn, ROPE_DIM // 2, axis=1) * s2)
    return outs[0] if len(outs) == 1 else jnp.concatenate(outs, axis=1)


def _sink_softmax_pv(s, sink, v_bf16):
    m = jnp.maximum(jnp.max(s, axis=-1, keepdims=True), sink)
    p = jnp.exp(s - m)
    den = jnp.sum(p, axis=-1, keepdims=True) + jnp.exp(sink - m)
    return _dot(p.astype(BF16), v_bf16) / den


def _swa_prompt_kernel(sink_ref, q_ref, k_ref, v_ref, cos_ref, s1_ref, s2_ref, qg_ref, kg_ref, ones_ref,
                       o_ref, kkeep_ref, kprev_ref, vprev_ref):
    i = pl.program_id(1)
    blk = q_ref.shape[0]

    @pl.when(i == 0)
    def _():
        kprev_ref[...] = jnp.zeros_like(kprev_ref)
        vprev_ref[...] = jnp.zeros_like(vprev_ref)

    cosf, s1, s2, ones = cos_ref[...], s1_ref[...], s2_ref[...], ones_ref[...]
    qh = _norm_rope(q_ref[...], qg_ref[...], cosf, s1, s2, ones).astype(BF16)
    kc = _norm_rope(k_ref[...], kg_ref[...], cosf, s1, s2, ones)
    v = v_ref[...]
    kcat = jnp.concatenate([kprev_ref[...], kc], axis=0).astype(BF16)
    vcat = jnp.concatenate([vprev_ref[...], v], axis=0).astype(BF16)
    r = lax.broadcasted_iota(jnp.int32, (blk, 2 * blk), 0)
    c = lax.broadcasted_iota(jnp.int32, (blk, 2 * blk), 1)
    valid = (c >= r) & (c <= r + WINDOW) & ((c >= blk) | (i > 0))
    d = SWA_HEAD_DIM
    scores = []
    for h in range(SWA_HEADS):
        g = h // SWA_GROUPS
        s = lax.dot_general(qh[:, h * d:(h + 1) * d], kcat[:, g * d:(g + 1) * d], (((1,), (1,)), ((), ())),
                            preferred_element_type=F32) * (d ** -0.5)
        scores.append(jnp.where(valid, s, -jnp.inf))
    outs = [_sink_softmax_pv(scores[h], sink_ref[h], vcat[:, (h // SWA_GROUPS) * d:(h // SWA_GROUPS + 1) * d])
            for h in range(SWA_HEADS)]
    o_ref[...] = jnp.concatenate(outs, axis=1)
    kprev_ref[...] = kc
    vprev_ref[...] = v

    @pl.when(i == pl.num_programs(1) - 1)
    def _():
        kkeep_ref[...] = kc


def _swa_prompt(q3, k3, v3, tables, qg, kg, sinks, ones):
    bsz, t, _ = q3.shape
    blk = WINDOW
    row = lambda width: pl.BlockSpec((None, blk, width), lambda b, i: (b, i, 0))
    tab = pl.BlockSpec((blk, LANES), lambda b, i: (i, 0))
    return pl.pallas_call(
        _swa_prompt_kernel,
        grid=(bsz, t // blk),
        in_specs=[pl.BlockSpec(memory_space=pltpu.SMEM), row(SWA_Q), row(SWA_KV), row(SWA_KV), tab, tab, tab,
                  _const_spec((1, LANES)), _const_spec((1, LANES)), _const_spec((LANES, LANES))],
        out_specs=[row(SWA_Q), pl.BlockSpec((None, blk, SWA_KV), lambda b, i: (b, 0, 0))],
        out_shape=[jax.ShapeDtypeStruct((bsz, t, SWA_Q), F32), jax.ShapeDtypeStruct((bsz, blk, SWA_KV), F32)],
        scratch_shapes=[pltpu.VMEM((blk, SWA_KV), F32), pltpu.VMEM((blk, SWA_KV), F32)],
        compiler_params=_params("arbitrary", "arbitrary"),
        name="swa_prompt",
    )(sinks, q3, k3, v3, *tables, qg, kg, ones)


def _qk_rope_kernel(q_ref, k_ref, v_ref, cos_ref, s1_ref, s2_ref, qg_ref, kg_ref, ones_ref,
                    q8_ref, ko_ref, kt_ref, vt_ref):
    cosf, s1, s2, ones = cos_ref[...], s1_ref[...], s2_ref[...], ones_ref[...]
    qh = _norm_rope(q_ref[...], qg_ref[...], cosf, s1, s2, ones)
    lane_group = lax.broadcasted_iota(jnp.int32, (qh.shape[0], LANES), 1) // SWA_HEAD_DIM
    for h in range(SWA_HEADS):
        m, parity = divmod(h, 2)
        g = h // SWA_GROUPS
        piece = qh[:, m * LANES:(m + 1) * LANES]
        if parity != g:
            piece = pltpu.roll(piece, SWA_HEAD_DIM, axis=1)
        q8_ref[:, h, :] = jnp.where(lane_group == g, piece, 0.0)
    kh = _norm_rope(k_ref[...], kg_ref[...], cosf, s1, s2, ones)
    ko_ref[...] = kh
    kt_ref[...] = kh.T
    vt_ref[...] = v_ref[...].T


def _qk_rope(q, k, v, tables, qg, kg, ones):
    n = q.shape[0]
    args = (q, k, v, *tables, qg, kg, ones)
    shapes = ((n, SWA_HEADS, SWA_KV), (n, SWA_KV), (SWA_KV, n), (SWA_KV, n))
    return pl.pallas_call(
        _qk_rope_kernel,
        grid=(1,),
        in_specs=[_const_spec(a.shape) for a in args],
        out_specs=[_whole_spec(s) for s in shapes],
        out_shape=[jax.ShapeDtypeStruct(s, F32) for s in shapes],
        compiler_params=_params("arbitrary"),
        name="qk_rope_sample",
    )(*args)


def _swa_sample_kernel(sink_ref, q8_ref, kn_ref, vn_ref, knt_ref, vnt_ref, ck_ref, cv_ref, o_ref, cko_ref, cvo_ref):
    d = SWA_HEAD_DIM
    scale = d ** -0.5
    nrows = q8_ref.shape[0]
    window = ck_ref.shape[-1]
    head = lax.broadcasted_iota(jnp.int32, (SWA_HEADS, 1), 0)
    lane_group = lax.broadcasted_iota(jnp.int32, (1, SWA_KV), 1) // d
    newest = lax.broadcasted_iota(jnp.int32, (SWA_KV, window), 1) == window - 1
    sink = jnp.zeros((SWA_HEADS, 1), F32)
    for h in range(SWA_HEADS):
        sink = jnp.where(head == h, sink_ref[h], sink)
    nt = (((1,), (1,)), ((), ()))
    q8 = [q8_ref[b] for b in range(nrows)]
    kn = [kn_ref[pl.ds(b, 1), :] for b in range(nrows)]
    vn = [vn_ref[pl.ds(b, 1), :] for b in range(nrows)]
    kt = [ck_ref[b].reshape(SWA_KV, window) for b in range(nrows)]
    vt = [cv_ref[b].reshape(SWA_KV, window) for b in range(nrows)]
    s_past = [_dot(q8[b].astype(BF16), kt[b].astype(BF16)) * scale for b in range(nrows)]
    s_new = [jnp.sum(q8[b] * kn[b], axis=-1, keepdims=True) * scale for b in range(nrows)]
    probs = []
    for b in range(nrows):
        mx = jnp.maximum(jnp.maximum(jnp.max(s_past[b], axis=-1, keepdims=True), s_new[b]), sink)
        p_past, p_new = jnp.exp(s_past[b] - mx), jnp.exp(s_new[b] - mx)
        den = jnp.sum(p_past, axis=-1, keepdims=True) + p_new + jnp.exp(sink - mx)
        probs.append((p_past, p_new, den))
    for b in range(nrows):
        p_past, p_new, den = probs[b]
        o = (lax.dot_general(p_past.astype(BF16), vt[b].astype(BF16), nt, preferred_element_type=F32)
             + p_new * vn[b]) / den
        pairs = []
        for m in range(SWA_HEADS // 2):
            g = (2 * m) // SWA_GROUPS
            first, second = o[2 * m:2 * m + 1, :], o[2 * m + 1:2 * m + 2, :]
            if g == 0:
                second = pltpu.roll(second, d, axis=1)
            else:
                first = pltpu.roll(first, d, axis=1)
            pairs.append(jnp.where(lane_group == 0, first, second))
        o_ref[pl.ds(b, 1), :] = jnp.concatenate(pairs, axis=1)
        for src, col_ref, dst_ref in ((kt[b], knt_ref, cko_ref), (vt[b], vnt_ref, cvo_ref)):
            slid = jnp.where(newest, col_ref[:, b:b + 1], pltpu.roll(src, window - 1, axis=1))
            dst_ref[b] = slid.reshape(SWA_KV_HEADS, d, window)


def _swa_sample(q8, kh, v, kh_t, v_t, cache_k, cache_v, sinks, tb):
    n = q8.shape[0]
    w = cache_k.shape[-1]
    cache = pl.BlockSpec((tb, SWA_KV_HEADS, SWA_HEAD_DIM, w), lambda i: (i, 0, 0, 0))
    row = lambda width: pl.BlockSpec((tb, width), lambda i: (i, 0))
    col = pl.BlockSpec((None, SWA_KV, tb), lambda i: (i, 0, 0))
    return pl.pallas_call(
        _swa_sample_kernel,
        grid=(n // tb,),
        in_specs=[pl.BlockSpec(memory_space=pltpu.SMEM),
                  pl.BlockSpec((tb, SWA_HEADS, SWA_KV), lambda i: (i, 0, 0)), row(SWA_KV), row(SWA_KV), col, col,
                  cache, cache],
        out_specs=[row(SWA_Q), cache, cache],
        out_shape=[jax.ShapeDtypeStruct((n, SWA_Q), F32),
                   jax.ShapeDtypeStruct(cache_k.shape, F32), jax.ShapeDtypeStruct(cache_v.shape, F32)],
        compiler_params=_params("arbitrary"),
        name="swa_sample",
    )(sinks, q8, kh, v, kh_t, v_t, cache_k, cache_v)


def _mem_kv_kernel(mem_ref, g_ref, w_ref, kg_ref, mk_ref, mv_ref):
    kv = _dot(_rms(mem_ref[...], g_ref[...]).astype(BF16), w_ref[...])
    kg = kg_ref[...]
    mk_ref[...] = jnp.concatenate(
        [_rms(kv[:, h * MEM_HEAD_DIM:(h + 1) * MEM_HEAD_DIM], kg) for h in range(MEM_HEADS)], axis=1)
    mv_ref[...] = kv[:, MEM_WIDTH:]


def _mem_kv(mem, gain, w_bf16, kgain):
    bsz, m, _ = mem.shape
    out = pl.BlockSpec((None, m, MEM_WIDTH), lambda b: (b, 0, 0))
    return pl.pallas_call(
        _mem_kv_kernel,
        grid=(bsz,),
        in_specs=[pl.BlockSpec((None, m, D_MODEL), lambda b: (b, 0, 0)), _const_spec((1, D_MODEL)),
                  _const_spec((D_MODEL, 2 * MEM_WIDTH)), _const_spec((1, MEM_HEAD_DIM))],
        out_specs=[out, out],
        out_shape=[jax.ShapeDtypeStruct((bsz, m, MEM_WIDTH), F32)] * 2,
        compiler_params=_params("arbitrary"),
        name="mem_kv",
    )(mem, gain, w_bf16, kgain)


def _mem_attn_prompt_kernel(xq_ref, mk_ref, mv_ref, qg_ref, o_ref):
    xq = xq_ref[...]
    qg = qg_ref[...]
    d = MEM_HEAD_DIM
    scores = []
    for h in range(MEM_HEADS):
        lanes = slice(h * d, (h + 1) * d)
        q = _rms(xq[:, lanes], qg).astype(BF16)
        scores.append(lax.dot_general(q, mk_ref[:, lanes].astype(BF16), (((1,), (1,)), ((), ())),
                                      preferred_element_type=F32) * (d ** -0.5))
    outs = []
    for h in range(MEM_HEADS):
        s = scores[h]
        p = jnp.exp(s - jnp.max(s, axis=-1, keepdims=True))
        den = jnp.sum(p, axis=-1, keepdims=True)
        outs.append(_dot(p.astype(BF16), mv_ref[:, h * d:(h + 1) * d].astype(BF16)) / den)
    o_ref[...] = jnp.concatenate(outs, axis=1)


def _mem_attn_prompt(xq3, mk, mv, qgain, tq):
    bsz, t, _ = xq3.shape
    m = mk.shape[1]
    mem = pl.BlockSpec((None, m, MEM_WIDTH), lambda b, i: (b, 0, 0))
    row = pl.BlockSpec((None, tq, MEM_WIDTH), lambda b, i: (b, i, 0))
    return pl.pallas_call(
        _mem_attn_prompt_kernel,
        grid=(bsz, t // tq),
        in_specs=[row, mem, mem, _const_spec((1, MEM_HEAD_DIM))],
        out_specs=row,
        out_shape=jax.ShapeDtypeStruct((bsz, t, MEM_WIDTH), F32),
        compiler_params=_params("arbitrary", "arbitrary"),
        name="mem_attn_prompt",
    )(xq3, mk, mv, qgain)


def _mem_attn_sample_kernel(xq_ref, mk_ref, mv_ref, qg_ref, o_ref):
    d = MEM_HEAD_DIM
    qg = qg_ref[...]
    for b in range(xq_ref.shape[0]):
        q = _rms(xq_ref[b], qg)
        s = jnp.sum(mk_ref[b] * q, axis=-1, keepdims=True) * (d ** -0.5)
        p = jnp.exp(s - jnp.max(s, axis=0, keepdims=True))
        o_ref[b] = jnp.sum(p * mv_ref[b], axis=0) / jnp.sum(p, axis=0)


def _mem_attn_sample(xq3, mk5, mv5, qgain, tb):
    _, n, m, _, _ = mk5.shape
    mem = pl.BlockSpec((None, tb, m, MEM_HEADS, MEM_HEAD_DIM), lambda i: (0, i, 0, 0, 0))
    row = pl.BlockSpec((tb, MEM_HEADS, MEM_HEAD_DIM), lambda i: (i, 0, 0))
    return pl.pallas_call(
        _mem_attn_sample_kernel,
        grid=(n // tb,),
        in_specs=[row, mem, mem, _const_spec((1, MEM_HEAD_DIM))],
        out_specs=row,
        out_shape=jax.ShapeDtypeStruct((n, MEM_HEADS, MEM_HEAD_DIM), F32),
        compiler_params=_params("arbitrary"),
        name="mem_attn_sample",
    )(xq3, mk5, mv5, qgain)


def _values_from_tiles(y_ref):
    tm = y_ref.shape[0]
    left = lax.broadcasted_iota(jnp.int32, (tm, LANES), 1) < RW_HEAD_DIM
    groups = [[], []]
    for m in range(RW_WIDTH // LANES):
        even, odd = y_ref[:, 2 * m, :], y_ref[:, 2 * m + 1, :]
        groups[0].append(jnp.where(left, even, pltpu.roll(odd, RW_HEAD_DIM, axis=1)))
        groups[1].append(jnp.where(left, pltpu.roll(even, RW_HEAD_DIM, axis=1), odd))
    return jnp.concatenate([jnp.concatenate(g, axis=1) for g in groups], axis=0)


def _merge_ffn_kernel(x_ref, y_ref, bonus_ref, grw_ref, ob_ref, oc_ref, nmix_ref, wg_ref, lnw_ref, lnb_ref,
                      ones_ref, wa_ref, wb_ref, wc_ref, wo_ref, nffn_ref, wu_ref, wd_ref, out_ref, *, y_tiles):
    nbatch, tm, _ = x_ref.shape
    rows = lambda ref: ref[...].reshape(nbatch * tm, ref.shape[-1])
    x = rows(x_ref)
    gates = jax.nn.sigmoid(_dot(_rms(x, nmix_ref[...]).astype(BF16), wg_ref[...]))
    y = _values_from_tiles(y_ref) if y_tiles else rows(y_ref)
    ones = ones_ref[...]
    inv_n = 1.0 / RW_HEAD_DIM
    yc = y - _segsum(y, ones) * inv_n
    var = _segsum(yc * yc, ones) * inv_n
    out_a = (yc * lax.rsqrt(var + GN_EPS) * lnw_ref[...] + lnb_ref[...] + rows(bonus_ref)) * rows(grw_ref)
    merged = (gates[:, :D_MODEL] * _dot(out_a.astype(BF16), wa_ref[...])
              + gates[:, D_MODEL:2 * D_MODEL] * _dot(rows(ob_ref).astype(BF16), wb_ref[...])
              + gates[:, 2 * D_MODEL:] * _dot(rows(oc_ref).astype(BF16), wc_ref[...]))
    x1 = x + _dot(merged.astype(BF16), wo_ref[...])
    up = _dot(_rms(x1, nffn_ref[...]).astype(BF16), wu_ref[...])
    act = jnp.square(jnp.maximum(up, 0.0))
    out_ref[...] = (x1 + _dot(act.astype(BF16), wd_ref[...])).reshape(nbatch, tm, D_MODEL)


def _merge_ffn(x, y, bonus, grw, ob, oc, weights, tm):
    nbatch, t, _ = x.shape
    y_tiles = y.ndim == 3 and y.shape[1:] == (8, LANES) and nbatch == 2
    row = lambda width: pl.BlockSpec((nbatch, tm, width), lambda i: (0, i, 0))
    y_spec = pl.BlockSpec((tm, 8, LANES), lambda i: (i, 0, 0)) if y_tiles else row(RW_WIDTH)
    return pl.pallas_call(
        functools.partial(_merge_ffn_kernel, y_tiles=y_tiles),
        grid=(t // tm,),
        in_specs=[row(D_MODEL), y_spec] + [row(RW_WIDTH)] * 4 + [_const_spec(w.shape) for w in weights],
        out_specs=row(D_MODEL),
        out_shape=jax.ShapeDtypeStruct((nbatch, t, D_MODEL), F32),
        compiler_params=_params("arbitrary"),
        name="merge_ffn",
    )(x, y, bonus, grw, ob, oc, *weights)


def _rope_tables(pos):
    half = ROPE_DIM // 2
    inv_freq = jnp.power(jnp.float32(ROPE_THETA), -jnp.arange(half, dtype=F32) * (2.0 / ROPE_DIM))
    ang = pos.astype(F32)[:, None] * inv_freq[None, :]
    cos, sin = jnp.cos(ang), jnp.sin(ang)
    n = pos.shape[0]
    rest = SWA_HEAD_DIM - ROPE_DIM
    z8, zr = jnp.zeros((n, half), F32), jnp.zeros((n, rest), F32)
    cosf = jnp.concatenate([cos, cos, jnp.ones((n, rest), F32)], axis=1)
    s1 = jnp.concatenate([-sin, z8, zr], axis=1)
    s2 = jnp.concatenate([z8, sin, zr], axis=1)
    return tuple(jnp.tile(tbl, (1, LANES // SWA_HEAD_DIM)) for tbl in (cosf, s1, s2))


def kernel(x_prompt, x_sample, state_rwkv, state_rwkv_shift, cache_swa_k, cache_swa_v, cache_mem_k, cache_mem_v, mem_prompt, norm_mix, w_in, rw_mu, rw_w0, rw_w2, rw_a0, rw_a2, rw_g2, rw_k_k, rw_k_a, rw_r_k, rw_ln_w, rw_ln_b, q_norm, k_norm, swa_sinks, mem_norm, w_mem_kv, xq_norm, xk_norm, w_br_a, w_br_b, w_br_c, w_out, norm_ffn, w_up, w_down):
    bsz, t, _ = x_prompt.shape
    nb = x_sample.shape[0]
    assert w_in.shape[0] == 1 and x_sample.shape[1] == 1

    row = lambda p: p.reshape(1, -1)
    w_ = RW_WIDTH
    vcols = slice(2 * w_, 3 * w_)

    def tile_order(a, axis=-1):
        a = jnp.moveaxis(a, axis, -1)
        a = a.reshape(a.shape[:-1] + (RW_HEADS, 8, 8)).swapaxes(-3, -2).reshape(a.shape)
        return jnp.moveaxis(a, -1, axis)

    def tile_order_vcols(a):
        return jnp.concatenate([a[..., :2 * w_], tile_order(a[..., vcols]), a[..., 3 * w_:]], axis=-1)

    w_proj = tile_order_vcols(w_in[0][:, :PROJ_COLS]).astype(BF16)
    w_gate = w_in[0][:, PROJ_COLS:].astype(BF16)
    zeros_lora = jnp.zeros((64, w_), F32)
    w2pad = jnp.concatenate([rw_w2[0], zeros_lora], axis=0)
    a2pad = jnp.concatenate([zeros_lora, rw_a2[0]], axis=0)
    hi_lo = lambda w: (w.astype(BF16), (w - w.astype(BF16).astype(F32)).astype(BF16))
    head_nat = jnp.arange(w_) // RW_HEAD_DIM
    head_tile = (jnp.arange(w_) // 8) % RW_HEADS
    ones_kk = (head_nat[:, None] == head_nat[None, :]).astype(BF16)
    ones_kv = (head_nat[:, None] == head_tile[None, :]).astype(BF16)
    ones_vv = (head_tile[:, None] == head_tile[None, :]).astype(BF16)
    rw = (row(tile_order_vcols(rw_mu[0])), row(rw_w0[0]), *hi_lo(w2pad), row(rw_a0[0]), *hi_lo(a2pad),
          *hi_lo(tile_order(rw_g2[0])), row(rw_k_k[0]), row(rw_k_a[0]), row(rw_r_k[0]), ones_kk, ones_kv)
    qg = jnp.tile(row(q_norm[0]), (1, LANES // SWA_HEAD_DIM))
    kg = jnp.tile(row(k_norm[0]), (1, LANES // SWA_HEAD_DIM))
    ones_swa = _block_ones(LANES, SWA_HEAD_DIM)
    sinks = swa_sinks[0]
    merge_w = (row(norm_mix[0]), w_gate, row(tile_order(rw_ln_w[0])), row(tile_order(rw_ln_b[0])), ones_vv,
               tile_order(w_br_a[0], axis=0).astype(BF16), w_br_b[0].astype(BF16), w_br_c[0].astype(BF16),
               w_out[0].astype(BF16), row(norm_ffn[0]), w_up[0].astype(BF16), w_down[0].astype(BF16))

    xp = x_prompt.reshape(bsz * t, D_MODEL)
    z, q, k, v, xq = _proj(xp, row(norm_mix[0]), w_proj, 1024)
    z3 = z.reshape(bsz, t, RW_COLS)
    ops, vv, grw, bonus = _rwkv_prep_prompt(z3, jnp.zeros((bsz, 1, RW_COLS), F32), rw, 256)
    yp, sp = _rwkv_scan(ops, vv, jnp.zeros((RW_HEAD_DIM, 8, LANES), F32))
    state_p = _unpack_scan_state(sp, bsz)

    tables_p = _rope_tables(jnp.arange(t, dtype=jnp.int32))
    k3, v3 = k.reshape(bsz, t, SWA_KV), v.reshape(bsz, t, SWA_KV)
    out_b, k_keep = _swa_prompt(q.reshape(bsz, t, SWA_Q), k3, v3, tables_p, qg, kg, sinks, ones_swa)
    v_keep = v3[:, t - WINDOW:]

    mk, mv = _mem_kv(mem_prompt, row(mem_norm[0]), w_mem_kv[0].astype(BF16), row(xk_norm[0]))
    out_c = _mem_attn_prompt(xq.reshape(bsz, t, MEM_WIDTH), mk, mv, row(xq_norm[0]), 512)

    y_prompt = _merge_ffn(x_prompt, yp, bonus, grw, out_b, out_c, merge_w, 128)

    xs = x_sample.reshape(nb, D_MODEL)
    zs, qs, ks, vs, xqs = _proj(xs, row(norm_mix[0]), w_proj, nb)
    ops_s, vrw_s, grw_s, bonus_s = _rwkv_prep_sample(zs, tile_order_vcols(state_rwkv_shift[0]), rw)
    state_t, y_t = _rwkv_step(jnp.transpose(state_rwkv[0], (1, 2, 3, 0)), jnp.swapaxes(ops_s, 1, 2),
                              tile_order(vrw_s).T)
    state_s = jnp.transpose(state_t, (3, 0, 1, 2))
    y_s = y_t.T

    past = cache_swa_k.shape[2]
    assert past <= WINDOW and past <= PAST_LEN
    tables_s = _rope_tables(jnp.full((1,), PAST_LEN, dtype=jnp.int32))
    q8_s, kh_s, kh_t, vs_t = _qk_rope(qs, ks, vs, tables_s, qg, kg, ones_swa)
    tb = 8
    cols = lambda a: jnp.transpose(a.reshape(SWA_KV, nb // tb, tb), (1, 0, 2))
    window_minor = lambda c: jnp.transpose(c[0], (0, 2, 3, 1))
    ob_s, ck_new, cv_new = _swa_sample(q8_s, kh_s, vs, cols(kh_t), cols(vs_t), window_minor(cache_swa_k),
                                       window_minor(cache_swa_v), sinks, tb)
    ck_new, cv_new = (jnp.transpose(c, (0, 3, 1, 2)) for c in (ck_new, cv_new))
    oc_s = _mem_attn_sample(xqs.reshape(nb, MEM_HEADS, MEM_HEAD_DIM), cache_mem_k, cache_mem_v,
                            row(xq_norm[0]), 8)
    y_sample = _merge_ffn(xs[None], tile_order(y_s.reshape(nb, w_))[None], bonus_s[None], grw_s[None],
                          ob_s.reshape(1, nb, SWA_Q), oc_s.reshape(1, nb, MEM_WIDTH), merge_w, nb)

    kv5 = lambda a, n_: a.reshape(1, n_, -1, SWA_KV_HEADS, SWA_HEAD_DIM)
    mem5 = lambda a: a.reshape(1, bsz, -1, MEM_HEADS, MEM_HEAD_DIM)
    return (y_prompt,
            y_sample.reshape(nb, 1, D_MODEL),
            state_p[None],
            tile_order_vcols(z3[:, t - 1])[None],
            kv5(k_keep, bsz), kv5(v_keep, bsz),
            mem5(mk), mem5(mv),
            state_s.reshape(1, nb, RW_HEADS, RW_HEAD_DIM, RW_HEAD_DIM),
            tile_order_vcols(zs)[None],
            kv5(ck_new, nb), kv5(cv_new, nb))
```

```python
import functools

import jax
import jax.numpy as jnp
from jax import lax
from jax.experimental import pallas as pl
from jax.experimental.pallas import tpu as pltpu

F32 = jnp.float32
BF16 = jnp.bfloat16

D_MODEL = 1024
RW_HEADS = 8
RW_HEAD_DIM = 64
RW_WIDTH = RW_HEADS * RW_HEAD_DIM
RW_COLS = 3 * RW_WIDTH + 64 + 64 + 128
GN_EPS = 64e-5
SWA_HEADS = 8
SWA_KV_HEADS = 2
SWA_GROUPS = SWA_HEADS // SWA_KV_HEADS
SWA_HEAD_DIM = 64
SWA_Q = SWA_HEADS * SWA_HEAD_DIM
SWA_KV = SWA_KV_HEADS * SWA_HEAD_DIM
WINDOW = 128
PAST_LEN = 8192
ROPE_THETA = 500000.0
ROPE_DIM = SWA_HEAD_DIM // 4
MEM_HEADS = 4
MEM_HEAD_DIM = 128
MEM_WIDTH = MEM_HEADS * MEM_HEAD_DIM
D_FF = 4 * D_MODEL
NORM_EPS = 1e-5
PROJ_COLS = RW_COLS + SWA_Q + 2 * SWA_KV + MEM_WIDTH

LANES = 128
VMEM_LIMIT = 52 * 1024 * 1024


def _params(*sem):
    return pltpu.CompilerParams(dimension_semantics=sem, vmem_limit_bytes=VMEM_LIMIT)


def _const_spec(shape):
    nd = len(shape)
    return pl.BlockSpec(shape, lambda *_: (0,) * nd, pipeline_mode=pl.Buffered(1))


def _whole_spec(shape):
    nd = len(shape)
    return pl.BlockSpec(shape, lambda *_: (0,) * nd)


def _rms(x, gain):
    return x * lax.rsqrt(jnp.mean(x * x, axis=-1, keepdims=True) + NORM_EPS) * gain


def _split(x):
    hi = x.astype(BF16)
    return hi, (x - hi.astype(F32)).astype(BF16)


def _dot(a, b):
    return jnp.dot(a, b, preferred_element_type=F32)


def _segsum(x, ones):
    hi, lo = _split(x)
    return _dot(hi, ones) + _dot(lo, ones)


def _dot3(x, w_hi, w_lo):
    hi, lo = _split(x)
    return _dot(hi, w_hi) + _dot(lo, w_hi) + _dot(hi, w_lo)


def _block_ones(n, seg):
    idx = jnp.arange(n) // seg
    return (idx[:, None] == idx[None, :]).astype(BF16)


def _proj_kernel(x_ref, g_ref, w_ref, z_ref, q_ref, k_ref, v_ref, xq_ref):
    h = _rms(x_ref[...], g_ref[...])
    p = _dot(h.astype(BF16), w_ref[...])
    c0, c1, c2, c3 = RW_COLS, RW_COLS + SWA_Q, RW_COLS + SWA_Q + SWA_KV, RW_COLS + SWA_Q + 2 * SWA_KV
    z_ref[...] = p[:, :c0]
    q_ref[...] = p[:, c0:c1]
    k_ref[...] = p[:, c1:c2]
    v_ref[...] = p[:, c2:c3]
    xq_ref[...] = p[:, c3:]


def _proj(x, gain, w_bf16, tm):
    n = x.shape[0]
    widths = (RW_COLS, SWA_Q, SWA_KV, SWA_KV, MEM_WIDTH)
    return pl.pallas_call(
        _proj_kernel,
        grid=(n // tm,),
        in_specs=[pl.BlockSpec((tm, D_MODEL), lambda i: (i, 0)),
                  _const_spec((1, D_MODEL)),
                  _const_spec((D_MODEL, PROJ_COLS))],
        out_specs=[pl.BlockSpec((tm, w), lambda i: (i, 0)) for w in widths],
        out_shape=[jax.ShapeDtypeStruct((n, w), F32) for w in widths],
        compiler_params=_params("arbitrary"),
        name="in_proj",
    )(x, gain, w_bf16)


def _rwkv_prep_math(z, zprev, mu, w0, w2h, w2l, a0, a2h, a2l, g2h, g2l, k_k, k_a, r_k, ones, ones_v):
    zs = z + (zprev - z) * mu
    w_ = RW_WIDTH
    r, k, v = zs[:, :w_], zs[:, w_:2 * w_], zs[:, 2 * w_:3 * w_]
    u = zs[:, 3 * w_:3 * w_ + LANES]
    gd = zs[:, 3 * w_ + LANES:]
    w_log = -jax.nn.softplus(-(w0 + _dot3(jnp.tanh(u), w2h, w2l))) - 0.5
    decay = jnp.exp(-jnp.exp(w_log))
    a = jax.nn.sigmoid(a0 + _dot3(u, a2h, a2l))
    g = _dot3(jax.nn.sigmoid(gd), g2h, g2l)
    kkr = k * k_k
    kk = kkr / jnp.maximum(jnp.sqrt(_segsum(kkr * kkr, ones)), 1e-12)
    kh = k * (1.0 + (a - 1.0) * k_a)
    bonus = _segsum(r * kh * r_k, ones_v) * v
    return kk, decay, kk * a, kh, r, v, g, bonus


def _proj_prep_prompt_kernel(x_ref, nmix_ref, w_ref, z0_ref, mu, w0, w2h, w2l, a0, a2h, a2l, g2h, g2l, k_k, k_a,
                             r_k, ones, ones_v, q_ref, k_ref, v_ref, xq_ref, zlast_ref, ops_ref, vv_ref, g_ref,
                             bonus_ref, nat_ref, carry_ref):
    nbatch, tm, _ = x_ref.shape
    half = tm // 2
    left_half = lax.broadcasted_iota(jnp.int32, (half, LANES), 1) < RW_HEAD_DIM
    left_full = lax.broadcasted_iota(jnp.int32, (tm, LANES), 1) < RW_HEAD_DIM
    row = lax.broadcasted_iota(jnp.int32, (tm, RW_COLS), 0)
    c0, c1, c2, c3 = RW_COLS, RW_COLS + SWA_Q, RW_COLS + SWA_Q + SWA_KV, RW_COLS + SWA_Q + 2 * SWA_KV

    @pl.when(pl.program_id(0) == 0)
    def _():
        carry_ref[...] = z0_ref[...]

    proj = [_dot(_rms(x_ref[b], nmix_ref[...]).astype(BF16), w_ref[...]) for b in range(nbatch)]
    values = []
    for b in range(nbatch):
        p = proj[b]
        q_ref[b] = p[:, c0:c1]
        k_ref[b] = p[:, c1:c2]
        v_ref[b] = p[:, c2:c3]
        xq_ref[b] = p[:, c3:]
        z = p[:, :c0]
        zprev = jnp.where(row == 0, carry_ref[b], pltpu.roll(z, 1, axis=0))
        carry_ref[b] = z[tm - 1:tm, :]
        zlast_ref[b] = z[tm - 1:tm, :]
        outs = _rwkv_prep_math(z, zprev, mu[...], w0[...], w2h[...], w2l[...], a0[...], a2h[...], a2l[...],
                               g2h[...], g2l[...], k_k[...], k_a[...], r_k[...], ones[...], ones_v[...])
        for j in range(SCAN_OPS):
            for m in range(RW_WIDTH // LANES):
                nat_ref[m] = outs[j][:, m * LANES:(m + 1) * LANES]
                ge = nat_ref[m, pl.ds(0, half, stride=2), :]
                go = nat_ref[m, pl.ds(1, half, stride=2), :]
                head = b * RW_HEADS + 2 * m
                ops_ref[j, head] = jnp.where(left_half, ge, pltpu.roll(go, RW_HEAD_DIM, axis=1))
                ops_ref[j, head + 1] = jnp.where(left_half, pltpu.roll(ge, RW_HEAD_DIM, axis=1), go)
        values.append(outs[5])
        g_ref[b] = outs[6]
        bonus_ref[b] = outs[7]
    assert nbatch == 2
    for vh in range(8):
        m, odd = divmod(vh, 2)
        g0, g1 = (val[:, m * LANES:(m + 1) * LANES] for val in values)
        if odd:
            vv_ref[:, vh, :] = jnp.where(left_full, pltpu.roll(g0, RW_HEAD_DIM, axis=1), g1)
        else:
            vv_ref[:, vh, :] = jnp.where(left_full, g0, pltpu.roll(g1, RW_HEAD_DIM, axis=1))


def _rwkv_prep_sample_kernel(z_ref, zp_ref, mu, w0, w2h, w2l, a0, a2h, a2l, g2h, g2l, k_k, k_a, r_k,
                             ones, ones_v, ops_ref, v_ref, g_ref, bonus_ref):
    outs = _rwkv_prep_math(z_ref[...], zp_ref[...], mu[...], w0[...], w2h[...], w2l[...], a0[...], a2h[...],
                           a2l[...], g2h[...], g2l[...], k_k[...], k_a[...], r_k[...], ones[...], ones_v[...])
    for j in range(5):
        ops_ref[j] = outs[j]
    v_ref[...] = outs[5]
    g_ref[...] = outs[6]
    bonus_ref[...] = outs[7]


def _proj_prep_prompt(x3, nmix, w_bf16, z0, rw, tm):
    bsz, t, _ = x3.shape
    w_ = RW_WIDTH
    row_spec = lambda width: pl.BlockSpec((bsz, tm, width), lambda i: (0, i, 0))
    last_spec = pl.BlockSpec((bsz, 1, RW_COLS), lambda i: (0, 0, 0))
    widths = (SWA_Q, SWA_KV, SWA_KV, MEM_WIDTH)
    return pl.pallas_call(
        _proj_prep_prompt_kernel,
        grid=(t // tm,),
        in_specs=[row_spec(D_MODEL), _const_spec((1, D_MODEL)), _const_spec((D_MODEL, PROJ_COLS)),
                  _const_spec((bsz, 1, RW_COLS))] + [_const_spec(p.shape) for p in rw],
        out_specs=[row_spec(w) for w in widths] + [
            last_spec,
            pl.BlockSpec((SCAN_OPS, bsz * RW_HEADS, tm // 2, LANES), lambda i: (0, 0, i, 0)),
            pl.BlockSpec((tm, 8, LANES), lambda i: (i, 0, 0)),
            row_spec(w_), row_spec(w_)],
        out_shape=[jax.ShapeDtypeStruct((bsz, t, w), F32) for w in widths] + [
            jax.ShapeDtypeStruct((bsz, 1, RW_COLS), F32),
            jax.ShapeDtypeStruct((SCAN_OPS, bsz * RW_HEADS, t // 2, LANES), F32),
            jax.ShapeDtypeStruct((t, 8, LANES), F32),
            jax.ShapeDtypeStruct((bsz, t, w_), F32), jax.ShapeDtypeStruct((bsz, t, w_), F32)],
        scratch_shapes=[pltpu.VMEM((w_ // LANES, tm, LANES), F32), pltpu.VMEM((bsz, 1, RW_COLS), F32)],
        compiler_params=_params("arbitrary"),
        name="proj_prep_prompt",
    )(x3, nmix, w_bf16, z0, *rw)


def _rwkv_prep_sample(z, zprev, rw):
    n = z.shape[0]
    w_ = RW_WIDTH
    return pl.pallas_call(
        _rwkv_prep_sample_kernel,
        grid=(1,),
        in_specs=[_const_spec((n, RW_COLS)), _const_spec((n, RW_COLS))] + [_const_spec(p.shape) for p in rw],
        out_specs=[_whole_spec((5, n, w_)), _whole_spec((n, w_)), _whole_spec((n, w_)), _whole_spec((n, w_))],
        out_shape=[jax.ShapeDtypeStruct((5, n, w_), F32)] + [jax.ShapeDtypeStruct((n, w_), F32)] * 3,
        compiler_params=_params("arbitrary"),
        name="rwkv_prep_sample",
    )(z, zprev, *rw)


SCAN_OPS = 5
SCAN_CHUNK = 32
N_BH = 16


def _tree_sum(parts):
    while len(parts) > 1:
        parts = [parts[i] + parts[i + 1] for i in range(0, len(parts), 2)]
    return parts[0]


def _rwkv_scan_kernel(opp_ref, oppn_ref, vv_ref, s0_ref, y_ref, sout_ref, s_ref, xr0_ref, xr1_ref):
    step = pl.program_id(0)
    tc = vv_ref.shape[0]
    nk = RW_HEAD_DIM

    def build_pair(src_ref, dst_ref, tp):
        for op in range(SCAN_OPS):
            rows = [jnp.broadcast_to(src_ref[op, bh, pl.ds(tp, 1), :], (8, LANES)) for bh in range(N_BH)]
            m_out = jnp.concatenate(rows, axis=0).T
            dst_ref[op, 2 * tp] = m_out[:nk]
            dst_ref[op, 2 * tp + 1] = m_out[nk:]

    @pl.when(step == 0)
    def _():
        s_ref[...] = s0_ref[...]

        def first(tp, carry):
            build_pair(opp_ref, xr0_ref, tp)
            return carry

        lax.fori_loop(0, tc // 2, first, 0)

    def run(cur_ref, nxt_ref):
        def token(t):
            def opnd(op, k):
                return jnp.broadcast_to(cur_ref[op, t, pl.ds(k, 1), :], (8, LANES))

            accs = [None] * 4
            for k in range(nk):
                p = s_ref[k] * opnd(0, k)
                accs[k % 4] = p if accs[k % 4] is None else accs[k % 4] + p
            sa = -_tree_sum(accs)
            vv = vv_ref[t]
            yacc = [None] * 4
            for k in range(nk):
                s_new = s_ref[k] * opnd(1, k) + sa * opnd(2, k) + vv * opnd(3, k)
                s_ref[k] = s_new
                p = s_new * opnd(4, k)
                yacc[k % 4] = p if yacc[k % 4] is None else yacc[k % 4] + p
            y_ref[t] = _tree_sum(yacc)

        def pair(tp, carry):
            build_pair(oppn_ref, nxt_ref, tp)
            token(2 * tp)
            token(2 * tp + 1)
            return carry

        lax.fori_loop(0, tc // 2, pair, 0)

    @pl.when(step % 2 == 0)
    def _():
        run(xr0_ref, xr1_ref)

    @pl.when(step % 2 == 1)
    def _():
        run(xr1_ref, xr0_ref)

    @pl.when(step == pl.num_programs(0) - 1)
    def _():
        sout_ref[...] = s_ref[...]


def _rwkv_scan(opp, vvp, s0p):
    t = vvp.shape[0]
    tc = SCAN_CHUNK
    nsteps = t // tc
    nk = RW_HEAD_DIM
    return pl.pallas_call(
        _rwkv_scan_kernel,
        grid=(nsteps,),
        in_specs=[pl.BlockSpec((SCAN_OPS, N_BH, tc // 2, LANES), lambda i: (0, 0, i, 0)),
                  pl.BlockSpec((SCAN_OPS, N_BH, tc // 2, LANES), lambda i: (0, 0, jnp.minimum(i + 1, nsteps - 1), 0)),
                  pl.BlockSpec((tc, 8, LANES), lambda i: (i, 0, 0)),
                  pl.BlockSpec((nk, 8, LANES), lambda i: (0, 0, 0))],
        out_specs=[pl.BlockSpec((tc, 8, LANES), lambda i: (i, 0, 0)),
                   pl.BlockSpec((nk, 8, LANES), lambda i: (0, 0, 0))],
        out_shape=[jax.ShapeDtypeStruct((t, 8, LANES), F32), jax.ShapeDtypeStruct((nk, 8, LANES), F32)],
        scratch_shapes=[pltpu.VMEM((nk, 8, LANES), F32),
                        pltpu.VMEM((SCAN_OPS, tc, nk, LANES), F32),
                        pltpu.VMEM((SCAN_OPS, tc, nk, LANES), F32)],
        compiler_params=_params("arbitrary"),
        name="rwkv_scan",
    )(opp, opp, vvp, s0p)


def _unpack_scan_state(sp, bsz):
    x = sp.reshape(RW_HEAD_DIM, 8, bsz, RW_HEADS, 8)
    return jnp.transpose(x, (2, 3, 1, 4, 0)).reshape(bsz, RW_HEADS, RW_HEAD_DIM, RW_HEAD_DIM)


def _rwkv_step_kernel(s_ref, ops_ref, v_ref, so_ref, y_ref):
    kk, w, b, kh, r = (ops_ref[j] for j in range(SCAN_OPS))
    for vi in range(s_ref.shape[0]):
        s = s_ref[vi]
        sa = -jnp.sum(s * kk, axis=0, keepdims=True)
        s_new = s * w + sa * b + v_ref[pl.ds(vi, 1), :] * kh
        so_ref[vi] = s_new
        y_ref[pl.ds(vi, 1), :] = jnp.sum(s_new * r, axis=0, keepdims=True)


def _rwkv_step(state_t, ops_t, v_t):
    nh, n, _, nbatch = state_t.shape
    vec = pl.BlockSpec((n, nbatch), lambda h: (h, 0))
    mat = pl.BlockSpec((None, n, n, nbatch), lambda h: (h, 0, 0, 0))
    return pl.pallas_call(
        _rwkv_step_kernel,
        grid=(nh,),
        in_specs=[mat, pl.BlockSpec((SCAN_OPS, n, nbatch), lambda h: (0, h, 0)), vec],
        out_specs=[mat, vec],
        out_shape=[jax.ShapeDtypeStruct(state_t.shape, F32), jax.ShapeDtypeStruct((nh * n, nbatch), F32)],
        compiler_params=_params("arbitrary"),
        name="rwkv_step",
    )(state_t, ops_t, v_t)


def _norm_rope(x, gain, cosf, s1, s2, ones):
    outs = []
    for j in range(x.shape[1] // LANES):
        xj = x[:, j * LANES:(j + 1) * LANES]
        ms = _segsum(xj * xj, ones) * (1.0 / SWA_HEAD_DIM)
        xn = xj * lax.rsqrt(ms + NORM_EPS) * gain
        outs.append(xn * cosf + pltpu.roll(xn, LANES - ROPE_DIM // 2, axis=1) * s1
                    + pltpu.roll(xn, ROPE_DIM // 2, axis=1) * s2)
    return outs[0] if len(outs) == 1 else jnp.concatenate(outs, axis=1)


def _sink_softmax_pv(s, sink, v_bf16):
    m = jnp.maximum(jnp.max(s, axis=-1, keepdims=True), sink)
    p = jnp.exp(s - m)
    den = jnp.sum(p, axis=-1, keepdims=True) + jnp.exp(sink - m)
    return _dot(p.astype(BF16), v_bf16) / den


def _swa_prompt_kernel(sink_ref, q_ref, k_ref, v_ref, cos_ref, s1_ref, s2_ref, qg_ref, kg_ref, ones_ref,
                       o_ref, kkeep_ref, kprev_ref, vprev_ref):
    i = pl.program_id(1)
    blk = q_ref.shape[0]

    @pl.when(i == 0)
    def _():
        kprev_ref[...] = jnp.zeros_like(kprev_ref)
        vprev_ref[...] = jnp.zeros_like(vprev_ref)

    cosf, s1, s2, ones = cos_ref[...], s1_ref[...], s2_ref[...], ones_ref[...]
    qh = _norm_rope(q_ref[...], qg_ref[...], cosf, s1, s2, ones).astype(BF16)
    kc = _norm_rope(k_ref[...], kg_ref[...], cosf, s1, s2, ones)
    v = v_ref[...]
    kcat = jnp.concatenate([kprev_ref[...], kc], axis=0).astype(BF16)
    vcat = jnp.concatenate([vprev_ref[...], v], axis=0).astype(BF16)
    r = lax.broadcasted_iota(jnp.int32, (blk, 2 * blk), 0)
    c = lax.broadcasted_iota(jnp.int32, (blk, 2 * blk), 1)
    valid = (c >= r) & (c <= r + WINDOW) & ((c >= blk) | (i > 0))
    d = SWA_HEAD_DIM
    scores = []
    for h in range(SWA_HEADS):
        g = h // SWA_GROUPS
        s = lax.dot_general(qh[:, h * d:(h + 1) * d], kcat[:, g * d:(g + 1) * d], (((1,), (1,)), ((), ())),
                            preferred_element_type=F32) * (d ** -0.5)
        scores.append(jnp.where(valid, s, -jnp.inf))
    outs = [_sink_softmax_pv(scores[h], sink_ref[h], vcat[:, (h // SWA_GROUPS) * d:(h // SWA_GROUPS + 1) * d])
            for h in range(SWA_HEADS)]
    o_ref[...] = jnp.concatenate(outs, axis=1)
    kprev_ref[...] = kc
    vprev_ref[...] = v

    @pl.when(i == pl.num_programs(1) - 1)
    def _():
        kkeep_ref[...] = kc


def _swa_prompt(q3, k3, v3, tables, qg, kg, sinks, ones):
    bsz, t, _ = q3.shape
    blk = WINDOW
    row = lambda width: pl.BlockSpec((None, blk, width), lambda b, i: (b, i, 0))
    tab = pl.BlockSpec((blk, LANES), lambda b, i: (i, 0))
    return pl.pallas_call(
        _swa_prompt_kernel,
        grid=(bsz, t // blk),
        in_specs=[pl.BlockSpec(memory_space=pltpu.SMEM), row(SWA_Q), row(SWA_KV), row(SWA_KV), tab, tab, tab,
                  _const_spec((1, LANES)), _const_spec((1, LANES)), _const_spec((LANES, LANES))],
        out_specs=[row(SWA_Q), pl.BlockSpec((None, blk, SWA_KV), lambda b, i: (b, 0, 0))],
        out_shape=[jax.ShapeDtypeStruct((bsz, t, SWA_Q), F32), jax.ShapeDtypeStruct((bsz, blk, SWA_KV), F32)],
        scratch_shapes=[pltpu.VMEM((blk, SWA_KV), F32), pltpu.VMEM((blk, SWA_KV), F32)],
        compiler_params=_params("arbitrary", "arbitrary"),
        name="swa_prompt",
    )(sinks, q3, k3, v3, *tables, qg, kg, ones)


def _qk_rope_kernel(q_ref, k_ref, v_ref, cos_ref, s1_ref, s2_ref, qg_ref, kg_ref, ones_ref,
                    q8_ref, ko_ref, kt_ref, vt_ref):
    cosf, s1, s2, ones = cos_ref[...], s1_ref[...], s2_ref[...], ones_ref[...]
    qh = _norm_rope(q_ref[...], qg_ref[...], cosf, s1, s2, ones)
    lane_group = lax.broadcasted_iota(jnp.int32, (qh.shape[0], LANES), 1) // SWA_HEAD_DIM
    for h in range(SWA_HEADS):
        m, parity = divmod(h, 2)
        g = h // SWA_GROUPS
        piece = qh[:, m * LANES:(m + 1) * LANES]
        if parity != g:
            piece = pltpu.roll(piece, SWA_HEAD_DIM, axis=1)
        q8_ref[:, h, :] = jnp.where(lane_group == g, piece, 0.0)
    kh = _norm_rope(k_ref[...], kg_ref[...], cosf, s1, s2, ones)
    ko_ref[...] = kh
    kt_ref[...] = kh.T
    vt_ref[...] = v_ref[...].T


def _qk_rope(q, k, v, tables, qg, kg, ones):
    n = q.shape[0]
    args = (q, k, v, *tables, qg, kg, ones)
    shapes = ((n, SWA_HEADS, SWA_KV), (n, SWA_KV), (SWA_KV, n), (SWA_KV, n))
    return pl.pallas_call(
        _qk_rope_kernel,
        grid=(1,),
        in_specs=[_const_spec(a.shape) for a in args],
        out_specs=[_whole_spec(s) for s in shapes],
        out_shape=[jax.ShapeDtypeStruct(s, F32) for s in shapes],
        compiler_params=_params("arbitrary"),
        name="qk_rope_sample",
    )(*args)


def _swa_sample_kernel(sink_ref, q8_ref, kn_ref, vn_ref, knt_ref, vnt_ref, ck_ref, cv_ref, o_ref, cko_ref, cvo_ref):
    d = SWA_HEAD_DIM
    scale = d ** -0.5
    nrows = q8_ref.shape[0]
    window = ck_ref.shape[-1]
    head = lax.broadcasted_iota(jnp.int32, (SWA_HEADS, 1), 0)
    lane_group = lax.broadcasted_iota(jnp.int32, (1, SWA_KV), 1) // d
    newest = lax.broadcasted_iota(jnp.int32, (SWA_KV, window), 1) == window - 1
    sink = jnp.zeros((SWA_HEADS, 1), F32)
    for h in range(SWA_HEADS):
        sink = jnp.where(head == h, sink_ref[h], sink)
    nt = (((1,), (1,)), ((), ()))
    q8 = [q8_ref[b] for b in range(nrows)]
    kn = [kn_ref[pl.ds(b, 1), :] for b in range(nrows)]
    vn = [vn_ref[pl.ds(b, 1), :] for b in range(nrows)]
    kt = [ck_ref[b].reshape(SWA_KV, window) for b in range(nrows)]
    vt = [cv_ref[b].reshape(SWA_KV, window) for b in range(nrows)]
    s_past = [_dot(q8[b].astype(BF16), kt[b].astype(BF16)) * scale for b in range(nrows)]
    s_new = [jnp.sum(q8[b] * kn[b], axis=-1, keepdims=True) * scale for b in range(nrows)]
    probs = []
    for b in range(nrows):
        mx = jnp.maximum(jnp.maximum(jnp.max(s_past[b], axis=-1, keepdims=True), s_new[b]), sink)
        p_past, p_new = jnp.exp(s_past[b] - mx), jnp.exp(s_new[b] - mx)
        den = jnp.sum(p_past, axis=-1, keepdims=True) + p_new + jnp.exp(sink - mx)
        probs.append((p_past, p_new, den))
    for b in range(nrows):
        p_past, p_new, den = probs[b]
        o = (lax.dot_general(p_past.astype(BF16), vt[b].astype(BF16), nt, preferred_element_type=F32)
             + p_new * vn[b]) / den
        pairs = []
        for m in range(SWA_HEADS // 2):
            g = (2 * m) // SWA_GROUPS
            first, second = o[2 * m:2 * m + 1, :], o[2 * m + 1:2 * m + 2, :]
            if g == 0:
                second = pltpu.roll(second, d, axis=1)
            else:
                first = pltpu.roll(first, d, axis=1)
            pairs.append(jnp.where(lane_group == 0, first, second))
        o_ref[pl.ds(b, 1), :] = jnp.concatenate(pairs, axis=1)
        for src, col_ref, dst_ref in ((kt[b], knt_ref, cko_ref), (vt[b], vnt_ref, cvo_ref)):
            slid = jnp.where(newest, col_ref[:, b:b + 1], pltpu.roll(src, window - 1, axis=1))
            dst_ref[b] = slid.reshape(SWA_KV_HEADS, d, window)


def _swa_sample(q8, kh, v, kh_t, v_t, cache_k, cache_v, sinks, tb):
    n = q8.shape[0]
    w = cache_k.shape[-1]
    cache = pl.BlockSpec((tb, SWA_KV_HEADS, SWA_HEAD_DIM, w), lambda i: (i, 0, 0, 0))
    row = lambda width: pl.BlockSpec((tb, width), lambda i: (i, 0))
    col = pl.BlockSpec((None, SWA_KV, tb), lambda i: (i, 0, 0))
    return pl.pallas_call(
        _swa_sample_kernel,
        grid=(n // tb,),
        in_specs=[pl.BlockSpec(memory_space=pltpu.SMEM),
                  pl.BlockSpec((tb, SWA_HEADS, SWA_KV), lambda i: (i, 0, 0)), row(SWA_KV), row(SWA_KV), col, col,
                  cache, cache],
        out_specs=[row(SWA_Q), cache, cache],
        out_shape=[jax.ShapeDtypeStruct((n, SWA_Q), F32),
                   jax.ShapeDtypeStruct(cache_k.shape, F32), jax.ShapeDtypeStruct(cache_v.shape, F32)],
        compiler_params=_params("arbitrary"),
        name="swa_sample",
    )(sinks, q8, kh, v, kh_t, v_t, cache_k, cache_v)


def _mem_kv_kernel(mem_ref, g_ref, w_ref, kg_ref, mk_ref, mv_ref):
    kv = _dot(_rms(mem_ref[...], g_ref[...]).astype(BF16), w_ref[...])
    kg = kg_ref[...]
    mk_ref[...] = jnp.concatenate(
        [_rms(kv[:, h * MEM_HEAD_DIM:(h + 1) * MEM_HEAD_DIM], kg) for h in range(MEM_HEADS)], axis=1)
    mv_ref[...] = kv[:, MEM_WIDTH:]


def _mem_kv(mem, gain, w_bf16, kgain):
    bsz, m, _ = mem.shape
    out = pl.BlockSpec((None, m, MEM_WIDTH), lambda b: (b, 0, 0))
    return pl.pallas_call(
        _mem_kv_kernel,
        grid=(bsz,),
        in_specs=[pl.BlockSpec((None, m, D_MODEL), lambda b: (b, 0, 0)), _const_spec((1, D_MODEL)),
                  _const_spec((D_MODEL, 2 * MEM_WIDTH)), _const_spec((1, MEM_HEAD_DIM))],
        out_specs=[out, out],
        out_shape=[jax.ShapeDtypeStruct((bsz, m, MEM_WIDTH), F32)] * 2,
        compiler_params=_params("arbitrary"),
        name="mem_kv",
    )(mem, gain, w_bf16, kgain)


def _mem_attn_prompt_kernel(xq_ref, mk_ref, mv_ref, qg_ref, o_ref):
    xq = xq_ref[...]
    qg = qg_ref[...]
    d = MEM_HEAD_DIM
    scores = []
    for h in range(MEM_HEADS):
        lanes = slice(h * d, (h + 1) * d)
        q = _rms(xq[:, lanes], qg).astype(BF16)
        scores.append(lax.dot_general(q, mk_ref[:, lanes].astype(BF16), (((1,), (1,)), ((), ())),
                                      preferred_element_type=F32) * (d ** -0.5))
    outs = []
    for h in range(MEM_HEADS):
        s = scores[h]
        p = jnp.exp(s - jnp.max(s, axis=-1, keepdims=True))
        den = jnp.sum(p, axis=-1, keepdims=True)
        outs.append(_dot(p.astype(BF16), mv_ref[:, h * d:(h + 1) * d].astype(BF16)) / den)
    o_ref[...] = jnp.concatenate(outs, axis=1)


def _mem_attn_prompt(xq3, mk, mv, qgain, tq):
    bsz, t, _ = xq3.shape
    m = mk.shape[1]
    mem = pl.BlockSpec((None, m, MEM_WIDTH), lambda b, i: (b, 0, 0))
    row = pl.BlockSpec((None, tq, MEM_WIDTH), lambda b, i: (b, i, 0))
    return pl.pallas_call(
        _mem_attn_prompt_kernel,
        grid=(bsz, t // tq),
        in_specs=[row, mem, mem, _const_spec((1, MEM_HEAD_DIM))],
        out_specs=row,
        out_shape=jax.ShapeDtypeStruct((bsz, t, MEM_WIDTH), F32),
        compiler_params=_params("arbitrary", "arbitrary"),
        name="mem_attn_prompt",
    )(xq3, mk, mv, qgain)


def _mem_attn_sample_kernel(xq_ref, mk_ref, mv_ref, qg_ref, o_ref):
    d = MEM_HEAD_DIM
    qg = qg_ref[...]
    for b in range(xq_ref.shape[0]):
        q = _rms(xq_ref[b], qg)
        s = jnp.sum(mk_ref[b] * q, axis=-1, keepdims=True) * (d ** -0.5)
        p = jnp.exp(s - jnp.max(s, axis=0, keepdims=True))
        o_ref[b] = jnp.sum(p * mv_ref[b], axis=0) / jnp.sum(p, axis=0)


def _mem_attn_sample(xq3, mk5, mv5, qgain, tb):
    _, n, m, _, _ = mk5.shape
    mem = pl.BlockSpec((None, tb, m, MEM_HEADS, MEM_HEAD_DIM), lambda i: (0, i, 0, 0, 0))
    row = pl.BlockSpec((tb, MEM_HEADS, MEM_HEAD_DIM), lambda i: (i, 0, 0))
    return pl.pallas_call(
        _mem_attn_sample_kernel,
        grid=(n // tb,),
        in_specs=[row, mem, mem, _const_spec((1, MEM_HEAD_DIM))],
        out_specs=row,
        out_shape=jax.ShapeDtypeStruct((n, MEM_HEADS, MEM_HEAD_DIM), F32),
        compiler_params=_params("arbitrary"),
        name="mem_attn_sample",
    )(xq3, mk5, mv5, qgain)


def _values_from_tiles(y_ref):
    tm = y_ref.shape[0]
    left = lax.broadcasted_iota(jnp.int32, (tm, LANES), 1) < RW_HEAD_DIM
    groups = [[], []]
    for m in range(RW_WIDTH // LANES):
        even, odd = y_ref[:, 2 * m, :], y_ref[:, 2 * m + 1, :]
        groups[0].append(jnp.where(left, even, pltpu.roll(odd, RW_HEAD_DIM, axis=1)))
        groups[1].append(jnp.where(left, pltpu.roll(even, RW_HEAD_DIM, axis=1), odd))
    return jnp.concatenate([jnp.concatenate(g, axis=1) for g in groups], axis=0)


def _merge_ffn_kernel(x_ref, y_ref, bonus_ref, grw_ref, ob_ref, oc_ref, nmix_ref, wg_ref, lnw_ref, lnb_ref,
                      ones_ref, wa_ref, wb_ref, wc_ref, wo_ref, nffn_ref, wu_ref, wd_ref, out_ref, *, y_tiles):
    nbatch, tm, _ = x_ref.shape
    rows = lambda ref: ref[...].reshape(nbatch * tm, ref.shape[-1])
    x = rows(x_ref)
    gates = jax.nn.sigmoid(_dot(_rms(x, nmix_ref[...]).astype(BF16), wg_ref[...]))
    y = _values_from_tiles(y_ref) if y_tiles else rows(y_ref)
    ones = ones_ref[...]
    inv_n = 1.0 / RW_HEAD_DIM
    yc = y - _segsum(y, ones) * inv_n
    var = _segsum(yc * yc, ones) * inv_n
    out_a = (yc * lax.rsqrt(var + GN_EPS) * lnw_ref[...] + lnb_ref[...] + rows(bonus_ref)) * rows(grw_ref)
    merged = (gates[:, :D_MODEL] * _dot(out_a.astype(BF16), wa_ref[...])
              + gates[:, D_MODEL:2 * D_MODEL] * _dot(rows(ob_ref).astype(BF16), wb_ref[...])
              + gates[:, 2 * D_MODEL:] * _dot(rows(oc_ref).astype(BF16), wc_ref[...]))
    x1 = x + _dot(merged.astype(BF16), wo_ref[...])
    up = _dot(_rms(x1, nffn_ref[...]).astype(BF16), wu_ref[...])
    act = jnp.square(jnp.maximum(up, 0.0))
    out_ref[...] = (x1 + _dot(act.astype(BF16), wd_ref[...])).reshape(nbatch, tm, D_MODEL)


def _merge_ffn(x, y, bonus, grw, ob, oc, weights, tm):
    nbatch, t, _ = x.shape
    y_tiles = y.ndim == 3 and y.shape[1:] == (8, LANES) and nbatch == 2
    row = lambda width: pl.BlockSpec((nbatch, tm, width), lambda i: (0, i, 0))
    y_spec = pl.BlockSpec((tm, 8, LANES), lambda i: (i, 0, 0)) if y_tiles else row(RW_WIDTH)
    return pl.pallas_call(
        functools.partial(_merge_ffn_kernel, y_tiles=y_tiles),
        grid=(t // tm,),
        in_specs=[row(D_MODEL), y_spec] + [row(RW_WIDTH)] * 4 + [_const_spec(w.shape) for w in weights],
        out_specs=row(D_MODEL),
        out_shape=jax.ShapeDtypeStruct((nbatch, t, D_MODEL), F32),
        compiler_params=_params("arbitrary"),
        name="merge_ffn",
    )(x, y, bonus, grw, ob, oc, *weights)


def _rope_tables(pos):
    half = ROPE_DIM // 2
    inv_freq = jnp.power(jnp.float32(ROPE_THETA), -jnp.arange(half, dtype=F32) * (2.0 / ROPE_DIM))
    ang = pos.astype(F32)[:, None] * inv_freq[None, :]
    cos, sin = jnp.cos(ang), jnp.sin(ang)
    n = pos.shape[0]
    rest = SWA_HEAD_DIM - ROPE_DIM
    z8, zr = jnp.zeros((n, half), F32), jnp.zeros((n, rest), F32)
    cosf = jnp.concatenate([cos, cos, jnp.ones((n, rest), F32)], axis=1)
    s1 = jnp.concatenate([-sin, z8, zr], axis=1)
    s2 = jnp.concatenate([z8, sin, zr], axis=1)
    return tuple(jnp.tile(tbl, (1, LANES // SWA_HEAD_DIM)) for tbl in (cosf, s1, s2))


def kernel(x_prompt, x_sample, state_rwkv, state_rwkv_shift, cache_swa_k, cache_swa_v, cache_mem_k, cache_mem_v, mem_prompt, norm_mix, w_in, rw_mu, rw_w0, rw_w2, rw_a0, rw_a2, rw_g2, rw_k_k, rw_k_a, rw_r_k, rw_ln_w, rw_ln_b, q_norm, k_norm, swa_sinks, mem_norm, w_mem_kv, xq_norm, xk_norm, w_br_a, w_br_b, w_br_c, w_out, norm_ffn, w_up, w_down):
    bsz, t, _ = x_prompt.shape
    nb = x_sample.shape[0]
    assert w_in.shape[0] == 1 and x_sample.shape[1] == 1

    row = lambda p: p.reshape(1, -1)
    w_ = RW_WIDTH
    vcols = slice(2 * w_, 3 * w_)

    def tile_order(a, axis=-1):
        a = jnp.moveaxis(a, axis, -1)
        a = a.reshape(a.shape[:-1] + (RW_HEADS, 8, 8)).swapaxes(-3, -2).reshape(a.shape)
        return jnp.moveaxis(a, -1, axis)

    def tile_order_vcols(a):
        return jnp.concatenate([a[..., :2 * w_], tile_order(a[..., vcols]), a[..., 3 * w_:]], axis=-1)

    w_proj = tile_order_vcols(w_in[0][:, :PROJ_COLS]).astype(BF16)
    w_gate = w_in[0][:, PROJ_COLS:].astype(BF16)
    zeros_lora = jnp.zeros((64, w_), F32)
    w2pad = jnp.concatenate([rw_w2[0], zeros_lora], axis=0)
    a2pad = jnp.concatenate([zeros_lora, rw_a2[0]], axis=0)
    hi_lo = lambda w: (w.astype(BF16), (w - w.astype(BF16).astype(F32)).astype(BF16))
    head_nat = jnp.arange(w_) // RW_HEAD_DIM
    head_tile = (jnp.arange(w_) // 8) % RW_HEADS
    ones_kk = (head_nat[:, None] == head_nat[None, :]).astype(BF16)
    ones_kv = (head_nat[:, None] == head_tile[None, :]).astype(BF16)
    ones_vv = (head_tile[:, None] == head_tile[None, :]).astype(BF16)
    rw = (row(tile_order_vcols(rw_mu[0])), row(rw_w0[0]), *hi_lo(w2pad), row(rw_a0[0]), *hi_lo(a2pad),
          *hi_lo(tile_order(rw_g2[0])), row(rw_k_k[0]), row(rw_k_a[0]), row(rw_r_k[0]), ones_kk, ones_kv)
    qg = jnp.tile(row(q_norm[0]), (1, LANES // SWA_HEAD_DIM))
    kg = jnp.tile(row(k_norm[0]), (1, LANES // SWA_HEAD_DIM))
    ones_swa = _block_ones(LANES, SWA_HEAD_DIM)
    sinks = swa_sinks[0]
    merge_w = (row(norm_mix[0]), w_gate, row(tile_order(rw_ln_w[0])), row(tile_order(rw_ln_b[0])), ones_vv,
               tile_order(w_br_a[0], axis=0).astype(BF16), w_br_b[0].astype(BF16), w_br_c[0].astype(BF16),
               w_out[0].astype(BF16), row(norm_ffn[0]), w_up[0].astype(BF16), w_down[0].astype(BF16))

    q, k3, v3, xq, z_last, ops, vv, grw, bonus = _proj_prep_prompt(
        x_prompt, row(norm_mix[0]), w_proj, jnp.zeros((bsz, 1, RW_COLS), F32), rw, 256)
    yp, sp = _rwkv_scan(ops, vv, jnp.zeros((RW_HEAD_DIM, 8, LANES), F32))
    state_p = _unpack_scan_state(sp, bsz)

    tables_p = _rope_tables(jnp.arange(t, dtype=jnp.int32))
    out_b, k_keep = _swa_prompt(q, k3, v3, tables_p, qg, kg, sinks, ones_swa)
    v_keep = v3[:, t - WINDOW:]

    mk, mv = _mem_kv(mem_prompt, row(mem_norm[0]), w_mem_kv[0].astype(BF16), row(xk_norm[0]))
    out_c = _mem_attn_prompt(xq, mk, mv, row(xq_norm[0]), 512)

    y_prompt = _merge_ffn(x_prompt, yp, bonus, grw, out_b, out_c, merge_w, 128)

    xs = x_sample.reshape(nb, D_MODEL)
    zs, qs, ks, vs, xqs = _proj(xs, row(norm_mix[0]), w_proj, nb)
    ops_s, vrw_s, grw_s, bonus_s = _rwkv_prep_sample(zs, tile_order_vcols(state_rwkv_shift[0]), rw)
    state_t, y_t = _rwkv_step(jnp.transpose(state_rwkv[0], (1, 2, 3, 0)), jnp.swapaxes(ops_s, 1, 2),
                              tile_order(vrw_s).T)
    state_s = jnp.transpose(state_t, (3, 0, 1, 2))
    y_s = y_t.T

    past = cache_swa_k.shape[2]
    assert past <= WINDOW and past <= PAST_LEN
    tables_s = _rope_tables(jnp.full((1,), PAST_LEN, dtype=jnp.int32))
    q8_s, kh_s, kh_t, vs_t = _qk_rope(qs, ks, vs, tables_s, qg, kg, ones_swa)
    tb = 8
    cols = lambda a: jnp.transpose(a.reshape(SWA_KV, nb // tb, tb), (1, 0, 2))
    window_minor = lambda c: jnp.transpose(c[0], (0, 2, 3, 1))
    ob_s, ck_new, cv_new = _swa_sample(q8_s, kh_s, vs, cols(kh_t), cols(vs_t), window_minor(cache_swa_k),
                                       window_minor(cache_swa_v), sinks, tb)
    ck_new, cv_new = (jnp.transpose(c, (0, 3, 1, 2)) for c in (ck_new, cv_new))
    oc_s = _mem_attn_sample(xqs.reshape(nb, MEM_HEADS, MEM_HEAD_DIM), cache_mem_k, cache_mem_v,
                            row(xq_norm[0]), 8)
    y_sample = _merge_ffn(xs[None], tile_order(y_s.reshape(nb, w_))[None], bonus_s[None], grw_s[None],
                          ob_s.reshape(1, nb, SWA_Q), oc_s.reshape(1, nb, MEM_WIDTH), merge_w, nb)

    kv5 = lambda a, n_: a.reshape(1, n_, -1, SWA_KV_HEADS, SWA_HEAD_DIM)
    mem5 = lambda a: a.reshape(1, bsz, -1, MEM_HEADS, MEM_HEAD_DIM)
    return (y_prompt,
            y_sample.reshape(nb, 1, D_MODEL),
            state_p[None],
            tile_order_vcols(z_last[:, 0])[None],
            kv5(k_keep, bsz), kv5(v_keep, bsz),
            mem5(mk), mem5(mv),
            state_s.reshape(1, nb, RW_HEADS, RW_HEAD_DIM, RW_HEAD_DIM),
            tile_order_vcols(zs)[None],
            kv5(ck_new, nb), kv5(cv_new, nb))
```

```python
import functools

import jax
import jax.numpy as jnp
from jax import lax
from jax.experimental import pallas as pl
from jax.experimental.pallas import tpu as pltpu

F32 = jnp.float32
BF16 = jnp.bfloat16

D_MODEL = 1024
RW_HEADS = 8
RW_HEAD_DIM = 64
RW_WIDTH = RW_HEADS * RW_HEAD_DIM
RW_COLS = 3 * RW_WIDTH + 64 + 64 + 128
GN_EPS = 64e-5
SWA_HEADS = 8
SWA_KV_HEADS = 2
SWA_GROUPS = SWA_HEADS // SWA_KV_HEADS
SWA_HEAD_DIM = 64
SWA_Q = SWA_HEADS * SWA_HEAD_DIM
SWA_KV = SWA_KV_HEADS * SWA_HEAD_DIM
WINDOW = 128
PAST_LEN = 8192
ROPE_THETA = 500000.0
ROPE_DIM = SWA_HEAD_DIM // 4
MEM_HEADS = 4
MEM_HEAD_DIM = 128
MEM_WIDTH = MEM_HEADS * MEM_HEAD_DIM
D_FF = 4 * D_MODEL
NORM_EPS = 1e-5
PROJ_COLS = RW_COLS + SWA_Q + 2 * SWA_KV + MEM_WIDTH

LANES = 128
VMEM_LIMIT = 52 * 1024 * 1024


def _params(*sem):
    return pltpu.CompilerParams(dimension_semantics=sem, vmem_limit_bytes=VMEM_LIMIT)


def _const_spec(shape):
    nd = len(shape)
    return pl.BlockSpec(shape, lambda *_: (0,) * nd, pipeline_mode=pl.Buffered(1))


def _whole_spec(shape):
    nd = len(shape)
    return pl.BlockSpec(shape, lambda *_: (0,) * nd)


def _rms(x, gain):
    return x * lax.rsqrt(jnp.mean(x * x, axis=-1, keepdims=True) + NORM_EPS) * gain


def _split(x):
    hi = x.astype(BF16)
    return hi, (x - hi.astype(F32)).astype(BF16)


def _dot(a, b):
    return jnp.dot(a, b, preferred_element_type=F32)


def _segsum(x, ones):
    hi, lo = _split(x)
    return _dot(hi, ones) + _dot(lo, ones)


def _dot3(x, w_hi, w_lo):
    hi, lo = _split(x)
    return _dot(hi, w_hi) + _dot(lo, w_hi) + _dot(hi, w_lo)


def _block_ones(n, seg):
    idx = jnp.arange(n) // seg
    return (idx[:, None] == idx[None, :]).astype(BF16)


def _proj_kernel(x_ref, g_ref, w_ref, z_ref, q_ref, k_ref, v_ref, xq_ref):
    h = _rms(x_ref[...], g_ref[...])
    p = _dot(h.astype(BF16), w_ref[...])
    c0, c1, c2, c3 = RW_COLS, RW_COLS + SWA_Q, RW_COLS + SWA_Q + SWA_KV, RW_COLS + SWA_Q + 2 * SWA_KV
    z_ref[...] = p[:, :c0]
    q_ref[...] = p[:, c0:c1]
    k_ref[...] = p[:, c1:c2]
    v_ref[...] = p[:, c2:c3]
    xq_ref[...] = p[:, c3:]


def _proj(x, gain, w_bf16, tm):
    n = x.shape[0]
    widths = (RW_COLS, SWA_Q, SWA_KV, SWA_KV, MEM_WIDTH)
    return pl.pallas_call(
        _proj_kernel,
        grid=(n // tm,),
        in_specs=[pl.BlockSpec((tm, D_MODEL), lambda i: (i, 0)),
                  _const_spec((1, D_MODEL)),
                  _const_spec((D_MODEL, PROJ_COLS))],
        out_specs=[pl.BlockSpec((tm, w), lambda i: (i, 0)) for w in widths],
        out_shape=[jax.ShapeDtypeStruct((n, w), F32) for w in widths],
        compiler_params=_params("arbitrary"),
        name="in_proj",
    )(x, gain, w_bf16)


def _rwkv_prep_math(z, zprev, mu, w0, w2h, w2l, a0, a2h, a2l, g2h, g2l, k_k, k_a, r_k, ones, ones_v):
    zs = z + (zprev - z) * mu
    w_ = RW_WIDTH
    r, k, v = zs[:, :w_], zs[:, w_:2 * w_], zs[:, 2 * w_:3 * w_]
    u = zs[:, 3 * w_:3 * w_ + LANES]
    gd = zs[:, 3 * w_ + LANES:]
    w_log = -jax.nn.softplus(-(w0 + _dot3(jnp.tanh(u), w2h, w2l))) - 0.5
    decay = jnp.exp(-jnp.exp(w_log))
    a = jax.nn.sigmoid(a0 + _dot3(u, a2h, a2l))
    g = _dot3(jax.nn.sigmoid(gd), g2h, g2l)
    kkr = k * k_k
    kk = kkr / jnp.maximum(jnp.sqrt(_segsum(kkr * kkr, ones)), 1e-12)
    kh = k * (1.0 + (a - 1.0) * k_a)
    bonus = _segsum(r * kh * r_k, ones_v) * v
    return kk, decay, kk * a, kh, r, v, g, bonus


def _proj_prep_prompt_kernel(x_ref, nmix_ref, w_ref, z0_ref, mu, w0, w2h, w2l, a0, a2h, a2l, g2h, g2l, k_k, k_a,
                             r_k, ones, ones_v, q_ref, k_ref, v_ref, xq_ref, zlast_ref, ops_ref, vv_ref, g_ref,
                             bonus_ref, nat_ref, carry_ref):
    nbatch, tm, _ = x_ref.shape
    half = tm // 2
    left_half = lax.broadcasted_iota(jnp.int32, (half, LANES), 1) < RW_HEAD_DIM
    left_full = lax.broadcasted_iota(jnp.int32, (tm, LANES), 1) < RW_HEAD_DIM
    row = lax.broadcasted_iota(jnp.int32, (tm, RW_COLS), 0)
    c0, c1, c2, c3 = RW_COLS, RW_COLS + SWA_Q, RW_COLS + SWA_Q + SWA_KV, RW_COLS + SWA_Q + 2 * SWA_KV

    @pl.when(pl.program_id(0) == 0)
    def _():
        carry_ref[...] = z0_ref[...]

    proj = [_dot(_rms(x_ref[b], nmix_ref[...]).astype(BF16), w_ref[...]) for b in range(nbatch)]
    values = []
    for b in range(nbatch):
        p = proj[b]
        q_ref[b] = p[:, c0:c1]
        k_ref[b] = p[:, c1:c2]
        v_ref[b] = p[:, c2:c3]
        xq_ref[b] = p[:, c3:]
        z = p[:, :c0]
        zprev = jnp.where(row == 0, carry_ref[b], pltpu.roll(z, 1, axis=0))
        carry_ref[b] = z[tm - 1:tm, :]
        zlast_ref[b] = z[tm - 1:tm, :]
        outs = _rwkv_prep_math(z, zprev, mu[...], w0[...], w2h[...], w2l[...], a0[...], a2h[...], a2l[...],
                               g2h[...], g2l[...], k_k[...], k_a[...], r_k[...], ones[...], ones_v[...])
        for j in range(SCAN_OPS):
            for m in range(RW_WIDTH // LANES):
                nat_ref[m] = outs[j][:, m * LANES:(m + 1) * LANES]
                ge = nat_ref[m, pl.ds(0, half, stride=2), :]
                go = nat_ref[m, pl.ds(1, half, stride=2), :]
                head = b * RW_HEADS + 2 * m
                ops_ref[j, head] = jnp.where(left_half, ge, pltpu.roll(go, RW_HEAD_DIM, axis=1))
                ops_ref[j, head + 1] = jnp.where(left_half, pltpu.roll(ge, RW_HEAD_DIM, axis=1), go)
        values.append(outs[5])
        g_ref[b] = outs[6]
        bonus_ref[b] = outs[7]
    assert nbatch == 2
    for vh in range(8):
        m, odd = divmod(vh, 2)
        g0, g1 = (val[:, m * LANES:(m + 1) * LANES] for val in values)
        if odd:
            vv_ref[:, vh, :] = jnp.where(left_full, pltpu.roll(g0, RW_HEAD_DIM, axis=1), g1)
        else:
            vv_ref[:, vh, :] = jnp.where(left_full, g0, pltpu.roll(g1, RW_HEAD_DIM, axis=1))


def _rwkv_prep_sample_kernel(z_ref, zp_ref, mu, w0, w2h, w2l, a0, a2h, a2l, g2h, g2l, k_k, k_a, r_k,
                             ones, ones_v, ops_ref, v_ref, g_ref, bonus_ref):
    outs = _rwkv_prep_math(z_ref[...], zp_ref[...], mu[...], w0[...], w2h[...], w2l[...], a0[...], a2h[...],
                           a2l[...], g2h[...], g2l[...], k_k[...], k_a[...], r_k[...], ones[...], ones_v[...])
    for j in range(5):
        ops_ref[j] = outs[j]
    v_ref[...] = outs[5]
    g_ref[...] = outs[6]
    bonus_ref[...] = outs[7]


def _proj_prep_prompt(x3, nmix, w_bf16, z0, rw, tm):
    bsz, t, _ = x3.shape
    w_ = RW_WIDTH
    row_spec = lambda width: pl.BlockSpec((bsz, tm, width), lambda i: (0, i, 0))
    last_spec = pl.BlockSpec((bsz, 1, RW_COLS), lambda i: (0, 0, 0))
    widths = (SWA_Q, SWA_KV, SWA_KV, MEM_WIDTH)
    return pl.pallas_call(
        _proj_prep_prompt_kernel,
        grid=(t // tm,),
        in_specs=[row_spec(D_MODEL), _const_spec((1, D_MODEL)), _const_spec((D_MODEL, PROJ_COLS)),
                  _const_spec((bsz, 1, RW_COLS))] + [_const_spec(p.shape) for p in rw],
        out_specs=[row_spec(w) for w in widths] + [
            last_spec,
            pl.BlockSpec((SCAN_OPS, bsz * RW_HEADS, tm // 2, LANES), lambda i: (0, 0, i, 0)),
            pl.BlockSpec((tm, 8, LANES), lambda i: (i, 0, 0)),
            row_spec(w_), row_spec(w_)],
        out_shape=[jax.ShapeDtypeStruct((bsz, t, w), F32) for w in widths] + [
            jax.ShapeDtypeStruct((bsz, 1, RW_COLS), F32),
            jax.ShapeDtypeStruct((SCAN_OPS, bsz * RW_HEADS, t // 2, LANES), F32),
            jax.ShapeDtypeStruct((t, 8, LANES), F32),
            jax.ShapeDtypeStruct((bsz, t, w_), F32), jax.ShapeDtypeStruct((bsz, t, w_), F32)],
        scratch_shapes=[pltpu.VMEM((w_ // LANES, tm, LANES), F32), pltpu.VMEM((bsz, 1, RW_COLS), F32)],
        compiler_params=_params("arbitrary"),
        name="proj_prep_prompt",
    )(x3, nmix, w_bf16, z0, *rw)


def _rwkv_prep_sample(z, zprev, rw):
    n = z.shape[0]
    w_ = RW_WIDTH
    return pl.pallas_call(
        _rwkv_prep_sample_kernel,
        grid=(1,),
        in_specs=[_const_spec((n, RW_COLS)), _const_spec((n, RW_COLS))] + [_const_spec(p.shape) for p in rw],
        out_specs=[_whole_spec((5, n, w_)), _whole_spec((n, w_)), _whole_spec((n, w_)), _whole_spec((n, w_))],
        out_shape=[jax.ShapeDtypeStruct((5, n, w_), F32)] + [jax.ShapeDtypeStruct((n, w_), F32)] * 3,
        compiler_params=_params("arbitrary"),
        name="rwkv_prep_sample",
    )(z, zprev, *rw)


SCAN_OPS = 5
SCAN_CHUNK = 32
N_BH = 16


def _tree_sum(parts):
    while len(parts) > 1:
        parts = [parts[i] + parts[i + 1] for i in range(0, len(parts), 2)]
    return parts[0]


def _rwkv_scan_kernel(opp_ref, oppn_ref, vv_ref, s0_ref, y_ref, sout_ref, s_ref, xr0_ref, xr1_ref):
    step = pl.program_id(0)
    tc = vv_ref.shape[0]
    nk = RW_HEAD_DIM

    def build_pair(src_ref, dst_ref, tp):
        for op in range(SCAN_OPS):
            rows = [jnp.broadcast_to(src_ref[op, bh, pl.ds(tp, 1), :], (8, LANES)) for bh in range(N_BH)]
            m_out = jnp.concatenate(rows, axis=0).T
            dst_ref[op, 2 * tp] = m_out[:nk]
            dst_ref[op, 2 * tp + 1] = m_out[nk:]

    @pl.when(step == 0)
    def _():
        s_ref[...] = s0_ref[...]

        def first(tp, carry):
            build_pair(opp_ref, xr0_ref, tp)
            return carry

        lax.fori_loop(0, tc // 2, first, 0)

    def run(cur_ref, nxt_ref):
        def token(t):
            def opnd(op, k):
                return jnp.broadcast_to(cur_ref[op, t, pl.ds(k, 1), :], (8, LANES))

            accs = [None] * 4
            for k in range(nk):
                p = s_ref[k] * opnd(0, k)
                accs[k % 4] = p if accs[k % 4] is None else accs[k % 4] + p
            sa = -_tree_sum(accs)
            vv = vv_ref[t]
            yacc = [None] * 4
            for k in range(nk):
                s_new = s_ref[k] * opnd(1, k) + sa * opnd(2, k) + vv * opnd(3, k)
                s_ref[k] = s_new
                p = s_new * opnd(4, k)
                yacc[k % 4] = p if yacc[k % 4] is None else yacc[k % 4] + p
            y_ref[t] = _tree_sum(yacc)

        def pair(tp, carry):
            build_pair(oppn_ref, nxt_ref, tp)
            token(2 * tp)
            token(2 * tp + 1)
            return carry

        lax.fori_loop(0, tc // 2, pair, 0)

    @pl.when(step % 2 == 0)
    def _():
        run(xr0_ref, xr1_ref)

    @pl.when(step % 2 == 1)
    def _():
        run(xr1_ref, xr0_ref)

    @pl.when(step == pl.num_programs(0) - 1)
    def _():
        sout_ref[...] = s_ref[...]


def _rwkv_scan(opp, vvp, s0p):
    t = vvp.shape[0]
    tc = SCAN_CHUNK
    nsteps = t // tc
    nk = RW_HEAD_DIM
    return pl.pallas_call(
        _rwkv_scan_kernel,
        grid=(nsteps,),
        in_specs=[pl.BlockSpec((SCAN_OPS, N_BH, tc // 2, LANES), lambda i: (0, 0, i, 0)),
                  pl.BlockSpec((SCAN_OPS, N_BH, tc // 2, LANES), lambda i: (0, 0, jnp.minimum(i + 1, nsteps - 1), 0)),
                  pl.BlockSpec((tc, 8, LANES), lambda i: (i, 0, 0)),
                  pl.BlockSpec((nk, 8, LANES), lambda i: (0, 0, 0))],
        out_specs=[pl.BlockSpec((tc, 8, LANES), lambda i: (i, 0, 0)),
                   pl.BlockSpec((nk, 8, LANES), lambda i: (0, 0, 0))],
        out_shape=[jax.ShapeDtypeStruct((t, 8, LANES), F32), jax.ShapeDtypeStruct((nk, 8, LANES), F32)],
        scratch_shapes=[pltpu.VMEM((nk, 8, LANES), F32),
                        pltpu.VMEM((SCAN_OPS, tc, nk, LANES), F32),
                        pltpu.VMEM((SCAN_OPS, tc, nk, LANES), F32)],
        compiler_params=_params("arbitrary"),
        name="rwkv_scan",
    )(opp, opp, vvp, s0p)


def _unpack_scan_state(sp, bsz):
    x = sp.reshape(RW_HEAD_DIM, 8, bsz, RW_HEADS, 8)
    return jnp.transpose(x, (2, 3, 1, 4, 0)).reshape(bsz, RW_HEADS, RW_HEAD_DIM, RW_HEAD_DIM)


def _rwkv_step_kernel(s_ref, ops_ref, v_ref, so_ref, y_ref):
    kk, w, b, kh, r = (ops_ref[j] for j in range(SCAN_OPS))
    for vi in range(s_ref.shape[0]):
        s = s_ref[vi]
        sa = -jnp.sum(s * kk, axis=0, keepdims=True)
        s_new = s * w + sa * b + v_ref[pl.ds(vi, 1), :] * kh
        so_ref[vi] = s_new
        y_ref[pl.ds(vi, 1), :] = jnp.sum(s_new * r, axis=0, keepdims=True)


def _rwkv_step(state_t, ops_t, v_t):
    nh, n, _, nbatch = state_t.shape
    vec = pl.BlockSpec((n, nbatch), lambda h: (h, 0))
    mat = pl.BlockSpec((None, n, n, nbatch), lambda h: (h, 0, 0, 0))
    return pl.pallas_call(
        _rwkv_step_kernel,
        grid=(nh,),
        in_specs=[mat, pl.BlockSpec((SCAN_OPS, n, nbatch), lambda h: (0, h, 0)), vec],
        out_specs=[mat, vec],
        out_shape=[jax.ShapeDtypeStruct(state_t.shape, F32), jax.ShapeDtypeStruct((nh * n, nbatch), F32)],
        compiler_params=_params("arbitrary"),
        name="rwkv_step",
    )(state_t, ops_t, v_t)


def _norm_rope(x, gain, cosf, s1, s2, ones):
    outs = []
    for j in range(x.shape[1] // LANES):
        xj = x[:, j * LANES:(j + 1) * LANES]
        ms = _segsum(xj * xj, ones) * (1.0 / SWA_HEAD_DIM)
        xn = xj * lax.rsqrt(ms + NORM_EPS) * gain
        outs.append(xn * cosf + pltpu.roll(xn, LANES - ROPE_DIM // 2, axis=1) * s1
                    + pltpu.roll(xn, ROPE_DIM // 2, axis=1) * s2)
    return outs[0] if len(outs) == 1 else jnp.concatenate(outs, axis=1)


SWA_BLOCKS_PER_STEP = 2


def _sink_softmax_pv(s, sink, v_bf16):
    m = jnp.maximum(jnp.max(s, axis=-1, keepdims=True), sink)
    p = jnp.exp(s - m)
    den = jnp.sum(p, axis=-1, keepdims=True) + jnp.exp(sink - m)
    return _dot(p.astype(BF16), v_bf16) / den


def _swa_prompt_kernel(sink_ref, q_ref, k_ref, v_ref, cos_ref, s1_ref, s2_ref, qg_ref, kg_ref, ones_ref,
                       o_ref, kkeep_ref, kprev_ref, vprev_ref):
    i = pl.program_id(1)
    blk = WINDOW
    nsub = q_ref.shape[0] // blk
    d = SWA_HEAD_DIM
    assert (d ** -0.5) == 0.125

    @pl.when(i == 0)
    def _():
        kprev_ref[...] = jnp.zeros_like(kprev_ref)
        vprev_ref[...] = jnp.zeros_like(vprev_ref)

    cosf, s1, s2, ones = cos_ref[...], s1_ref[...], s2_ref[...], ones_ref[...]
    qh = (_norm_rope(q_ref[...], qg_ref[...], cosf, s1, s2, ones) * (d ** -0.5)).astype(BF16)
    kc = _norm_rope(k_ref[...], kg_ref[...], cosf, s1, s2, ones)
    v = v_ref[...]
    r = lax.broadcasted_iota(jnp.int32, (blk, 2 * blk), 0)
    c = lax.broadcasted_iota(jnp.int32, (blk, 2 * blk), 1)
    in_band = (c >= r) & (c <= r + WINDOW)
    kprev, vprev = kprev_ref[...], vprev_ref[...]
    for sub in range(nsub):
        rows = slice(sub * blk, (sub + 1) * blk)
        kcat = jnp.concatenate([kprev, kc[rows]], axis=0).astype(BF16)
        vcat = jnp.concatenate([vprev, v[rows]], axis=0).astype(BF16)
        valid = in_band & ((c >= blk) | (i > 0)) if sub == 0 else in_band
        scores = []
        for h in range(SWA_HEADS):
            g = h // SWA_GROUPS
            s = lax.dot_general(qh[rows, h * d:(h + 1) * d], kcat[:, g * d:(g + 1) * d], (((1,), (1,)), ((), ())),
                                preferred_element_type=F32)
            scores.append(jnp.where(valid, s, -jnp.inf))
        outs = [_sink_softmax_pv(scores[h], sink_ref[h], vcat[:, (h // SWA_GROUPS) * d:(h // SWA_GROUPS + 1) * d])
                for h in range(SWA_HEADS)]
        o_ref[rows, :] = jnp.concatenate(outs, axis=1)
        kprev, vprev = kc[rows], v[rows]
    kprev_ref[...] = kprev
    vprev_ref[...] = vprev

    @pl.when(i == pl.num_programs(1) - 1)
    def _():
        kkeep_ref[...] = kprev


def _swa_prompt(q3, k3, v3, tables, qg, kg, sinks, ones):
    bsz, t, _ = q3.shape
    blk = WINDOW
    rows = SWA_BLOCKS_PER_STEP * blk
    row = lambda width: pl.BlockSpec((None, rows, width), lambda b, i: (b, i, 0))
    tab = pl.BlockSpec((rows, LANES), lambda b, i: (i, 0))
    return pl.pallas_call(
        _swa_prompt_kernel,
        grid=(bsz, t // rows),
        in_specs=[pl.BlockSpec(memory_space=pltpu.SMEM), row(SWA_Q), row(SWA_KV), row(SWA_KV), tab, tab, tab,
                  _const_spec((1, LANES)), _const_spec((1, LANES)), _const_spec((LANES, LANES))],
        out_specs=[row(SWA_Q), pl.BlockSpec((None, blk, SWA_KV), lambda b, i: (b, 0, 0))],
        out_shape=[jax.ShapeDtypeStruct((bsz, t, SWA_Q), F32), jax.ShapeDtypeStruct((bsz, blk, SWA_KV), F32)],
        scratch_shapes=[pltpu.VMEM((blk, SWA_KV), F32), pltpu.VMEM((blk, SWA_KV), F32)],
        compiler_params=_params("arbitrary", "arbitrary"),
        name="swa_prompt",
    )(sinks, q3, k3, v3, *tables, qg, kg, ones)


def _qk_rope_kernel(q_ref, k_ref, v_ref, cos_ref, s1_ref, s2_ref, qg_ref, kg_ref, ones_ref,
                    q8_ref, ko_ref, kt_ref, vt_ref):
    cosf, s1, s2, ones = cos_ref[...], s1_ref[...], s2_ref[...], ones_ref[...]
    qh = _norm_rope(q_ref[...], qg_ref[...], cosf, s1, s2, ones)
    lane_group = lax.broadcasted_iota(jnp.int32, (qh.shape[0], LANES), 1) // SWA_HEAD_DIM
    for h in range(SWA_HEADS):
        m, parity = divmod(h, 2)
        g = h // SWA_GROUPS
        piece = qh[:, m * LANES:(m + 1) * LANES]
        if parity != g:
            piece = pltpu.roll(piece, SWA_HEAD_DIM, axis=1)
        q8_ref[:, h, :] = jnp.where(lane_group == g, piece, 0.0)
    kh = _norm_rope(k_ref[...], kg_ref[...], cosf, s1, s2, ones)
    ko_ref[...] = kh
    kt_ref[...] = kh.T
    vt_ref[...] = v_ref[...].T


def _qk_rope(q, k, v, tables, qg, kg, ones):
    n = q.shape[0]
    args = (q, k, v, *tables, qg, kg, ones)
    shapes = ((n, SWA_HEADS, SWA_KV), (n, SWA_KV), (SWA_KV, n), (SWA_KV, n))
    return pl.pallas_call(
        _qk_rope_kernel,
        grid=(1,),
        in_specs=[_const_spec(a.shape) for a in args],
        out_specs=[_whole_spec(s) for s in shapes],
        out_shape=[jax.ShapeDtypeStruct(s, F32) for s in shapes],
        compiler_params=_params("arbitrary"),
        name="qk_rope_sample",
    )(*args)


def _swa_sample_kernel(sink_ref, q8_ref, kn_ref, vn_ref, knt_ref, vnt_ref, ck_ref, cv_ref, o_ref, cko_ref, cvo_ref):
    d = SWA_HEAD_DIM
    scale = d ** -0.5
    nrows = q8_ref.shape[0]
    window = ck_ref.shape[-1]
    head = lax.broadcasted_iota(jnp.int32, (SWA_HEADS, 1), 0)
    lane_group = lax.broadcasted_iota(jnp.int32, (1, SWA_KV), 1) // d
    newest = lax.broadcasted_iota(jnp.int32, (SWA_KV, window), 1) == window - 1
    sink = jnp.zeros((SWA_HEADS, 1), F32)
    for h in range(SWA_HEADS):
        sink = jnp.where(head == h, sink_ref[h], sink)
    nt = (((1,), (1,)), ((), ()))
    q8 = [q8_ref[b] for b in range(nrows)]
    kn = [kn_ref[pl.ds(b, 1), :] for b in range(nrows)]
    vn = [vn_ref[pl.ds(b, 1), :] for b in range(nrows)]
    kt = [ck_ref[b].reshape(SWA_KV, window) for b in range(nrows)]
    vt = [cv_ref[b].reshape(SWA_KV, window) for b in range(nrows)]
    s_past = [_dot(q8[b].astype(BF16), kt[b].astype(BF16)) * scale for b in range(nrows)]
    s_new = [jnp.sum(q8[b] * kn[b], axis=-1, keepdims=True) * scale for b in range(nrows)]
    probs = []
    for b in range(nrows):
        mx = jnp.maximum(jnp.maximum(jnp.max(s_past[b], axis=-1, keepdims=True), s_new[b]), sink)
        p_past, p_new = jnp.exp(s_past[b] - mx), jnp.exp(s_new[b] - mx)
        den = jnp.sum(p_past, axis=-1, keepdims=True) + p_new + jnp.exp(sink - mx)
        probs.append((p_past, p_new, den))
    for b in range(nrows):
        p_past, p_new, den = probs[b]
        o = (lax.dot_general(p_past.astype(BF16), vt[b].astype(BF16), nt, preferred_element_type=F32)
             + p_new * vn[b]) / den
        pairs = []
        for m in range(SWA_HEADS // 2):
            g = (2 * m) // SWA_GROUPS
            first, second = o[2 * m:2 * m + 1, :], o[2 * m + 1:2 * m + 2, :]
            if g == 0:
                second = pltpu.roll(second, d, axis=1)
            else:
                first = pltpu.roll(first, d, axis=1)
            pairs.append(jnp.where(lane_group == 0, first, second))
        o_ref[pl.ds(b, 1), :] = jnp.concatenate(pairs, axis=1)
        for src, col_ref, dst_ref in ((kt[b], knt_ref, cko_ref), (vt[b], vnt_ref, cvo_ref)):
            slid = jnp.where(newest, col_ref[:, b:b + 1], pltpu.roll(src, window - 1, axis=1))
            dst_ref[b] = slid.reshape(SWA_KV_HEADS, d, window)


def _swa_sample(q8, kh, v, kh_t, v_t, cache_k, cache_v, sinks, tb):
    n = q8.shape[0]
    w = cache_k.shape[-1]
    cache = pl.BlockSpec((tb, SWA_KV_HEADS, SWA_HEAD_DIM, w), lambda i: (i, 0, 0, 0))
    row = lambda width: pl.BlockSpec((tb, width), lambda i: (i, 0))
    col = pl.BlockSpec((None, SWA_KV, tb), lambda i: (i, 0, 0))
    return pl.pallas_call(
        _swa_sample_kernel,
        grid=(n // tb,),
        in_specs=[pl.BlockSpec(memory_space=pltpu.SMEM),
                  pl.BlockSpec((tb, SWA_HEADS, SWA_KV), lambda i: (i, 0, 0)), row(SWA_KV), row(SWA_KV), col, col,
                  cache, cache],
        out_specs=[row(SWA_Q), cache, cache],
        out_shape=[jax.ShapeDtypeStruct((n, SWA_Q), F32),
                   jax.ShapeDtypeStruct(cache_k.shape, F32), jax.ShapeDtypeStruct(cache_v.shape, F32)],
        compiler_params=_params("arbitrary"),
        name="swa_sample",
    )(sinks, q8, kh, v, kh_t, v_t, cache_k, cache_v)


def _mem_kv_kernel(mem_ref, g_ref, w_ref, kg_ref, mk_ref, mv_ref):
    kv = _dot(_rms(mem_ref[...], g_ref[...]).astype(BF16), w_ref[...])
    kg = kg_ref[...]
    mk_ref[...] = jnp.concatenate(
        [_rms(kv[:, h * MEM_HEAD_DIM:(h + 1) * MEM_HEAD_DIM], kg) for h in range(MEM_HEADS)], axis=1)
    mv_ref[...] = kv[:, MEM_WIDTH:]


def _mem_kv(mem, gain, w_bf16, kgain):
    bsz, m, _ = mem.shape
    out = pl.BlockSpec((None, m, MEM_WIDTH), lambda b: (b, 0, 0))
    return pl.pallas_call(
        _mem_kv_kernel,
        grid=(bsz,),
        in_specs=[pl.BlockSpec((None, m, D_MODEL), lambda b: (b, 0, 0)), _const_spec((1, D_MODEL)),
                  _const_spec((D_MODEL, 2 * MEM_WIDTH)), _const_spec((1, MEM_HEAD_DIM))],
        out_specs=[out, out],
        out_shape=[jax.ShapeDtypeStruct((bsz, m, MEM_WIDTH), F32)] * 2,
        compiler_params=_params("arbitrary"),
        name="mem_kv",
    )(mem, gain, w_bf16, kgain)


def _mem_attn_prompt_kernel(xq_ref, mk_ref, mv_ref, qg_ref, o_ref):
    xq = xq_ref[...]
    qg = qg_ref[...]
    d = MEM_HEAD_DIM
    scores = []
    for h in range(MEM_HEADS):
        lanes = slice(h * d, (h + 1) * d)
        q = _rms(xq[:, lanes], qg).astype(BF16)
        scores.append(lax.dot_general(q, mk_ref[:, lanes].astype(BF16), (((1,), (1,)), ((), ())),
                                      preferred_element_type=F32) * (d ** -0.5))
    outs = []
    for h in range(MEM_HEADS):
        s = scores[h]
        p = jnp.exp(s - jnp.max(s, axis=-1, keepdims=True))
        den = jnp.sum(p, axis=-1, keepdims=True)
        outs.append(_dot(p.astype(BF16), mv_ref[:, h * d:(h + 1) * d].astype(BF16)) / den)
    o_ref[...] = jnp.concatenate(outs, axis=1)


def _mem_attn_prompt(xq3, mk, mv, qgain, tq):
    bsz, t, _ = xq3.shape
    m = mk.shape[1]
    mem = pl.BlockSpec((None, m, MEM_WIDTH), lambda b, i: (b, 0, 0))
    row = pl.BlockSpec((None, tq, MEM_WIDTH), lambda b, i: (b, i, 0))
    return pl.pallas_call(
        _mem_attn_prompt_kernel,
        grid=(bsz, t // tq),
        in_specs=[row, mem, mem, _const_spec((1, MEM_HEAD_DIM))],
        out_specs=row,
        out_shape=jax.ShapeDtypeStruct((bsz, t, MEM_WIDTH), F32),
        compiler_params=_params("arbitrary", "arbitrary"),
        name="mem_attn_prompt",
    )(xq3, mk, mv, qgain)


def _mem_attn_sample_kernel(xq_ref, mk_ref, mv_ref, qg_ref, o_ref):
    d = MEM_HEAD_DIM
    qg = qg_ref[...]
    for b in range(xq_ref.shape[0]):
        q = _rms(xq_ref[b], qg)
        s = jnp.sum(mk_ref[b] * q, axis=-1, keepdims=True) * (d ** -0.5)
        p = jnp.exp(s - jnp.max(s, axis=0, keepdims=True))
        o_ref[b] = jnp.sum(p * mv_ref[b], axis=0) / jnp.sum(p, axis=0)


def _mem_attn_sample(xq3, mk5, mv5, qgain, tb):
    _, n, m, _, _ = mk5.shape
    mem = pl.BlockSpec((None, tb, m, MEM_HEADS, MEM_HEAD_DIM), lambda i: (0, i, 0, 0, 0))
    row = pl.BlockSpec((tb, MEM_HEADS, MEM_HEAD_DIM), lambda i: (i, 0, 0))
    return pl.pallas_call(
        _mem_attn_sample_kernel,
        grid=(n // tb,),
        in_specs=[row, mem, mem, _const_spec((1, MEM_HEAD_DIM))],
        out_specs=row,
        out_shape=jax.ShapeDtypeStruct((n, MEM_HEADS, MEM_HEAD_DIM), F32),
        compiler_params=_params("arbitrary"),
        name="mem_attn_sample",
    )(xq3, mk5, mv5, qgain)


def _values_from_tiles(y_ref):
    tm = y_ref.shape[0]
    left = lax.broadcasted_iota(jnp.int32, (tm, LANES), 1) < RW_HEAD_DIM
    groups = [[], []]
    for m in range(RW_WIDTH // LANES):
        even, odd = y_ref[:, 2 * m, :], y_ref[:, 2 * m + 1, :]
        groups[0].append(jnp.where(left, even, pltpu.roll(odd, RW_HEAD_DIM, axis=1)))
        groups[1].append(jnp.where(left, pltpu.roll(even, RW_HEAD_DIM, axis=1), odd))
    return jnp.concatenate([jnp.concatenate(g, axis=1) for g in groups], axis=0)


def _merge_ffn_kernel(x_ref, y_ref, bonus_ref, grw_ref, ob_ref, oc_ref, nmix_ref, wg_ref, lnw_ref, lnb_ref,
                      ones_ref, wa_ref, wb_ref, wc_ref, wo_ref, nffn_ref, wu_ref, wd_ref, out_ref, *, y_tiles):
    nbatch, tm, _ = x_ref.shape
    rows = lambda ref: ref[...].reshape(nbatch * tm, ref.shape[-1])
    x = rows(x_ref)
    gates = jax.nn.sigmoid(_dot(_rms(x, nmix_ref[...]).astype(BF16), wg_ref[...]))
    y = _values_from_tiles(y_ref) if y_tiles else rows(y_ref)
    ones = ones_ref[...]
    inv_n = 1.0 / RW_HEAD_DIM
    yc = y - _segsum(y, ones) * inv_n
    var = _segsum(yc * yc, ones) * inv_n
    out_a = (yc * lax.rsqrt(var + GN_EPS) * lnw_ref[...] + lnb_ref[...] + rows(bonus_ref)) * rows(grw_ref)
    merged = (gates[:, :D_MODEL] * _dot(out_a.astype(BF16), wa_ref[...])
              + gates[:, D_MODEL:2 * D_MODEL] * _dot(rows(ob_ref).astype(BF16), wb_ref[...])
              + gates[:, 2 * D_MODEL:] * _dot(rows(oc_ref).astype(BF16), wc_ref[...]))
    x1 = x + _dot(merged.astype(BF16), wo_ref[...])
    up = _dot(_rms(x1, nffn_ref[...]).astype(BF16), wu_ref[...])
    act = jnp.square(jnp.maximum(up, 0.0))
    out_ref[...] = (x1 + _dot(act.astype(BF16), wd_ref[...])).reshape(nbatch, tm, D_MODEL)


def _merge_ffn(x, y, bonus, grw, ob, oc, weights, tm):
    nbatch, t, _ = x.shape
    y_tiles = y.ndim == 3 and y.shape[1:] == (8, LANES) and nbatch == 2
    row = lambda width: pl.BlockSpec((nbatch, tm, width), lambda i: (0, i, 0))
    y_spec = pl.BlockSpec((tm, 8, LANES), lambda i: (i, 0, 0)) if y_tiles else row(RW_WIDTH)
    return pl.pallas_call(
        functools.partial(_merge_ffn_kernel, y_tiles=y_tiles),
        grid=(t // tm,),
        in_specs=[row(D_MODEL), y_spec] + [row(RW_WIDTH)] * 4 + [_const_spec(w.shape) for w in weights],
        out_specs=row(D_MODEL),
        out_shape=jax.ShapeDtypeStruct((nbatch, t, D_MODEL), F32),
        compiler_params=_params("arbitrary"),
        name="merge_ffn",
    )(x, y, bonus, grw, ob, oc, *weights)


def _rope_tables(pos):
    half = ROPE_DIM // 2
    inv_freq = jnp.power(jnp.float32(ROPE_THETA), -jnp.arange(half, dtype=F32) * (2.0 / ROPE_DIM))
    ang = pos.astype(F32)[:, None] * inv_freq[None, :]
    cos, sin = jnp.cos(ang), jnp.sin(ang)
    n = pos.shape[0]
    rest = SWA_HEAD_DIM - ROPE_DIM
    z8, zr = jnp.zeros((n, half), F32), jnp.zeros((n, rest), F32)
    cosf = jnp.concatenate([cos, cos, jnp.ones((n, rest), F32)], axis=1)
    s1 = jnp.concatenate([-sin, z8, zr], axis=1)
    s2 = jnp.concatenate([z8, sin, zr], axis=1)
    return tuple(jnp.tile(tbl, (1, LANES // SWA_HEAD_DIM)) for tbl in (cosf, s1, s2))


def kernel(x_prompt, x_sample, state_rwkv, state_rwkv_shift, cache_swa_k, cache_swa_v, cache_mem_k, cache_mem_v, mem_prompt, norm_mix, w_in, rw_mu, rw_w0, rw_w2, rw_a0, rw_a2, rw_g2, rw_k_k, rw_k_a, rw_r_k, rw_ln_w, rw_ln_b, q_norm, k_norm, swa_sinks, mem_norm, w_mem_kv, xq_norm, xk_norm, w_br_a, w_br_b, w_br_c, w_out, norm_ffn, w_up, w_down):
    bsz, t, _ = x_prompt.shape
    nb = x_sample.shape[0]
    assert w_in.shape[0] == 1 and x_sample.shape[1] == 1

    row = lambda p: p.reshape(1, -1)
    w_ = RW_WIDTH
    vcols = slice(2 * w_, 3 * w_)

    def tile_order(a, axis=-1):
        a = jnp.moveaxis(a, axis, -1)
        a = a.reshape(a.shape[:-1] + (RW_HEADS, 8, 8)).swapaxes(-3, -2).reshape(a.shape)
        return jnp.moveaxis(a, -1, axis)

    def tile_order_vcols(a):
        return jnp.concatenate([a[..., :2 * w_], tile_order(a[..., vcols]), a[..., 3 * w_:]], axis=-1)

    w_proj = tile_order_vcols(w_in[0][:, :PROJ_COLS]).astype(BF16)
    w_gate = w_in[0][:, PROJ_COLS:].astype(BF16)
    zeros_lora = jnp.zeros((64, w_), F32)
    w2pad = jnp.concatenate([rw_w2[0], zeros_lora], axis=0)
    a2pad = jnp.concatenate([zeros_lora, rw_a2[0]], axis=0)
    hi_lo = lambda w: (w.astype(BF16), (w - w.astype(BF16).astype(F32)).astype(BF16))
    head_nat = jnp.arange(w_) // RW_HEAD_DIM
    head_tile = (jnp.arange(w_) // 8) % RW_HEADS
    ones_kk = (head_nat[:, None] == head_nat[None, :]).astype(BF16)
    ones_kv = (head_nat[:, None] == head_tile[None, :]).astype(BF16)
    ones_vv = (head_tile[:, None] == head_tile[None, :]).astype(BF16)
    rw = (row(tile_order_vcols(rw_mu[0])), row(rw_w0[0]), *hi_lo(w2pad), row(rw_a0[0]), *hi_lo(a2pad),
          *hi_lo(tile_order(rw_g2[0])), row(rw_k_k[0]), row(rw_k_a[0]), row(rw_r_k[0]), ones_kk, ones_kv)
    qg = jnp.tile(row(q_norm[0]), (1, LANES // SWA_HEAD_DIM))
    kg = jnp.tile(row(k_norm[0]), (1, LANES // SWA_HEAD_DIM))
    ones_swa = _block_ones(LANES, SWA_HEAD_DIM)
    sinks = swa_sinks[0]
    merge_w = (row(norm_mix[0]), w_gate, row(tile_order(rw_ln_w[0])), row(tile_order(rw_ln_b[0])), ones_vv,
               tile_order(w_br_a[0], axis=0).astype(BF16), w_br_b[0].astype(BF16), w_br_c[0].astype(BF16),
               w_out[0].astype(BF16), row(norm_ffn[0]), w_up[0].astype(BF16), w_down[0].astype(BF16))

    q, k3, v3, xq, z_last, ops, vv, grw, bonus = _proj_prep_prompt(
        x_prompt, row(norm_mix[0]), w_proj, jnp.zeros((bsz, 1, RW_COLS), F32), rw, 256)
    yp, sp = _rwkv_scan(ops, vv, jnp.zeros((RW_HEAD_DIM, 8, LANES), F32))
    state_p = _unpack_scan_state(sp, bsz)

    tables_p = _rope_tables(jnp.arange(t, dtype=jnp.int32))
    out_b, k_keep = _swa_prompt(q, k3, v3, tables_p, qg, kg, sinks, ones_swa)
    v_keep = v3[:, t - WINDOW:]

    mk, mv = _mem_kv(mem_prompt, row(mem_norm[0]), w_mem_kv[0].astype(BF16), row(xk_norm[0]))
    out_c = _mem_attn_prompt(xq, mk, mv, row(xq_norm[0]), 512)

    y_prompt = _merge_ffn(x_prompt, yp, bonus, grw, out_b, out_c, merge_w, 128)

    xs = x_sample.reshape(nb, D_MODEL)
    zs, qs, ks, vs, xqs = _proj(xs, row(norm_mix[0]), w_proj, nb)
    ops_s, vrw_s, grw_s, bonus_s = _rwkv_prep_sample(zs, tile_order_vcols(state_rwkv_shift[0]), rw)
    state_t, y_t = _rwkv_step(jnp.transpose(state_rwkv[0], (1, 2, 3, 0)), jnp.swapaxes(ops_s, 1, 2),
                              tile_order(vrw_s).T)
    state_s = jnp.transpose(state_t, (3, 0, 1, 2))
    y_s = y_t.T

    past = cache_swa_k.shape[2]
    assert past <= WINDOW and past <= PAST_LEN
    tables_s = _rope_tables(jnp.full((1,), PAST_LEN, dtype=jnp.int32))
    q8_s, kh_s, kh_t, vs_t = _qk_rope(qs, ks, vs, tables_s, qg, kg, ones_swa)
    tb = 8
    cols = lambda a: jnp.transpose(a.reshape(SWA_KV, nb // tb, tb), (1, 0, 2))
    window_minor = lambda c: jnp.transpose(c[0], (0, 2, 3, 1))
    ob_s, ck_new, cv_new = _swa_sample(q8_s, kh_s, vs, cols(kh_t), cols(vs_t), window_minor(cache_swa_k),
                                       window_minor(cache_swa_v), sinks, tb)
    ck_new, cv_new = (jnp.transpose(c, (0, 3, 1, 2)) for c in (ck_new, cv_new))
    oc_s = _mem_attn_sample(xqs.reshape(nb, MEM_HEADS, MEM_HEAD_DIM), cache_mem_k, cache_mem_v,
                            row(xq_norm[0]), 8)
    y_sample = _merge_ffn(xs[None], tile_order(y_s.reshape(nb, w_))[None], bonus_s[None], grw_s[None],
                          ob_s.reshape(1, nb, SWA_Q), oc_s.reshape(1, nb, MEM_WIDTH), merge_w, nb)

    kv5 = lambda a, n_: a.reshape(1, n_, -1, SWA_KV_HEADS, SWA_HEAD_DIM)
    mem5 = lambda a: a.reshape(1, bsz, -1, MEM_HEADS, MEM_HEAD_DIM)
    return (y_prompt,
            y_sample.reshape(nb, 1, D_MODEL),
            state_p[None],
            tile_order_vcols(z_last[:, 0])[None],
            kv5(k_keep, bsz), kv5(v_keep, bsz),
            mem5(mk), mem5(mv),
            state_s.reshape(1, nb, RW_HEADS, RW_HEAD_DIM, RW_HEAD_DIM),
            tile_order_vcols(zs)[None],
            kv5(ck_new, nb), kv5(cv_new, nb))
```

```python
import functools

import jax
import jax.numpy as jnp
from jax import lax
from jax.experimental import pallas as pl
from jax.experimental.pallas import tpu as pltpu

F32 = jnp.float32
BF16 = jnp.bfloat16

D_MODEL = 1024
RW_HEADS = 8
RW_HEAD_DIM = 64
RW_WIDTH = RW_HEADS * RW_HEAD_DIM
RW_COLS = 3 * RW_WIDTH + 64 + 64 + 128
GN_EPS = 64e-5
SWA_HEADS = 8
SWA_KV_HEADS = 2
SWA_GROUPS = SWA_HEADS // SWA_KV_HEADS
SWA_HEAD_DIM = 64
SWA_Q = SWA_HEADS * SWA_HEAD_DIM
SWA_KV = SWA_KV_HEADS * SWA_HEAD_DIM
WINDOW = 128
PAST_LEN = 8192
ROPE_THETA = 500000.0
ROPE_DIM = SWA_HEAD_DIM // 4
MEM_HEADS = 4
MEM_HEAD_DIM = 128
MEM_WIDTH = MEM_HEADS * MEM_HEAD_DIM
D_FF = 4 * D_MODEL
NORM_EPS = 1e-5
PROJ_COLS = RW_COLS + SWA_Q + 2 * SWA_KV + MEM_WIDTH

LANES = 128
VMEM_LIMIT = 52 * 1024 * 1024


def _params(*sem):
    return pltpu.CompilerParams(dimension_semantics=sem, vmem_limit_bytes=VMEM_LIMIT)


def _const_spec(shape):
    nd = len(shape)
    return pl.BlockSpec(shape, lambda *_: (0,) * nd, pipeline_mode=pl.Buffered(1))


def _whole_spec(shape):
    nd = len(shape)
    return pl.BlockSpec(shape, lambda *_: (0,) * nd)


def _rms(x, gain):
    return x * lax.rsqrt(jnp.mean(x * x, axis=-1, keepdims=True) + NORM_EPS) * gain


def _split(x):
    hi = x.astype(BF16)
    return hi, (x - hi.astype(F32)).astype(BF16)


def _dot(a, b):
    return jnp.dot(a, b, preferred_element_type=F32)


def _segsum(x, ones):
    hi, lo = _split(x)
    return _dot(hi, ones) + _dot(lo, ones)


def _dot3(x, w_hi, w_lo):
    hi, lo = _split(x)
    return _dot(hi, w_hi) + _dot(lo, w_hi) + _dot(hi, w_lo)


def _block_ones(n, seg):
    idx = jnp.arange(n) // seg
    return (idx[:, None] == idx[None, :]).astype(BF16)


def _proj_kernel(x_ref, g_ref, w_ref, z_ref, q_ref, k_ref, v_ref, xq_ref):
    h = _rms(x_ref[...], g_ref[...])
    p = _dot(h.astype(BF16), w_ref[...])
    c0, c1, c2, c3 = RW_COLS, RW_COLS + SWA_Q, RW_COLS + SWA_Q + SWA_KV, RW_COLS + SWA_Q + 2 * SWA_KV
    z_ref[...] = p[:, :c0]
    q_ref[...] = p[:, c0:c1]
    k_ref[...] = p[:, c1:c2]
    v_ref[...] = p[:, c2:c3]
    xq_ref[...] = p[:, c3:]


def _proj(x, gain, w_bf16, tm):
    n = x.shape[0]
    widths = (RW_COLS, SWA_Q, SWA_KV, SWA_KV, MEM_WIDTH)
    return pl.pallas_call(
        _proj_kernel,
        grid=(n // tm,),
        in_specs=[pl.BlockSpec((tm, D_MODEL), lambda i: (i, 0)),
                  _const_spec((1, D_MODEL)),
                  _const_spec((D_MODEL, PROJ_COLS))],
        out_specs=[pl.BlockSpec((tm, w), lambda i: (i, 0)) for w in widths],
        out_shape=[jax.ShapeDtypeStruct((n, w), F32) for w in widths],
        compiler_params=_params("arbitrary"),
        name="in_proj",
    )(x, gain, w_bf16)


def _rwkv_prep_math(z, zprev, mu, w0, w2h, w2l, a0, a2h, a2l, g2h, g2l, k_k, k_a, r_k, ones, ones_v):
    zs = z + (zprev - z) * mu
    w_ = RW_WIDTH
    r, k, v = zs[:, :w_], zs[:, w_:2 * w_], zs[:, 2 * w_:3 * w_]
    u = zs[:, 3 * w_:3 * w_ + LANES]
    gd = zs[:, 3 * w_ + LANES:]
    w_log = -jax.nn.softplus(-(w0 + _dot3(jnp.tanh(u), w2h, w2l))) - 0.5
    decay = jnp.exp(-jnp.exp(w_log))
    a = jax.nn.sigmoid(a0 + _dot3(u, a2h, a2l))
    g = _dot3(jax.nn.sigmoid(gd), g2h, g2l)
    kkr = k * k_k
    kk = kkr / jnp.maximum(jnp.sqrt(_segsum(kkr * kkr, ones)), 1e-12)
    kh = k * (1.0 + (a - 1.0) * k_a)
    bonus = _segsum(r * kh * r_k, ones_v) * v
    return kk, decay, kk * a, kh, r, v, g, bonus


def _proj_prep_prompt_kernel(x_ref, nmix_ref, w_ref, z0_ref, mu, w0, w2h, w2l, a0, a2h, a2l, g2h, g2l, k_k, k_a,
                             r_k, ones, ones_v, q_ref, k_ref, v_ref, xq_ref, zlast_ref, ops_ref, vv_ref, g_ref,
                             bonus_ref, nat_ref, carry_ref):
    nbatch, tm, _ = x_ref.shape
    half = tm // 2
    left_half = lax.broadcasted_iota(jnp.int32, (half, LANES), 1) < RW_HEAD_DIM
    left_full = lax.broadcasted_iota(jnp.int32, (tm, LANES), 1) < RW_HEAD_DIM
    row = lax.broadcasted_iota(jnp.int32, (tm, RW_COLS), 0)
    c0, c1, c2, c3 = RW_COLS, RW_COLS + SWA_Q, RW_COLS + SWA_Q + SWA_KV, RW_COLS + SWA_Q + 2 * SWA_KV

    @pl.when(pl.program_id(0) == 0)
    def _():
        carry_ref[...] = z0_ref[...]

    proj = [_dot(_rms(x_ref[b], nmix_ref[...]).astype(BF16), w_ref[...]) for b in range(nbatch)]
    values = []
    for b in range(nbatch):
        p = proj[b]
        q_ref[b] = p[:, c0:c1]
        k_ref[b] = p[:, c1:c2]
        v_ref[b] = p[:, c2:c3]
        xq_ref[b] = p[:, c3:]
        z = p[:, :c0]
        zprev = jnp.where(row == 0, carry_ref[b], pltpu.roll(z, 1, axis=0))
        carry_ref[b] = z[tm - 1:tm, :]
        zlast_ref[b] = z[tm - 1:tm, :]
        outs = _rwkv_prep_math(z, zprev, mu[...], w0[...], w2h[...], w2l[...], a0[...], a2h[...], a2l[...],
                               g2h[...], g2l[...], k_k[...], k_a[...], r_k[...], ones[...], ones_v[...])
        for j in range(SCAN_OPS):
            for m in range(RW_WIDTH // LANES):
                nat_ref[m] = outs[j][:, m * LANES:(m + 1) * LANES]
                ge = nat_ref[m, pl.ds(0, half, stride=2), :]
                go = nat_ref[m, pl.ds(1, half, stride=2), :]
                head = b * RW_HEADS + 2 * m
                ops_ref[j, head] = jnp.where(left_half, ge, pltpu.roll(go, RW_HEAD_DIM, axis=1))
                ops_ref[j, head + 1] = jnp.where(left_half, pltpu.roll(ge, RW_HEAD_DIM, axis=1), go)
        values.append(outs[5])
        g_ref[b] = outs[6]
        bonus_ref[b] = outs[7]
    assert nbatch == 2
    for vh in range(8):
        m, odd = divmod(vh, 2)
        g0, g1 = (val[:, m * LANES:(m + 1) * LANES] for val in values)
        if odd:
            vv_ref[:, vh, :] = jnp.where(left_full, pltpu.roll(g0, RW_HEAD_DIM, axis=1), g1)
        else:
            vv_ref[:, vh, :] = jnp.where(left_full, g0, pltpu.roll(g1, RW_HEAD_DIM, axis=1))


def _rwkv_prep_sample_kernel(z_ref, zp_ref, mu, w0, w2h, w2l, a0, a2h, a2l, g2h, g2l, k_k, k_a, r_k,
                             ones, ones_v, ops_ref, v_ref, g_ref, bonus_ref):
    outs = _rwkv_prep_math(z_ref[...], zp_ref[...], mu[...], w0[...], w2h[...], w2l[...], a0[...], a2h[...],
                           a2l[...], g2h[...], g2l[...], k_k[...], k_a[...], r_k[...], ones[...], ones_v[...])
    for j in range(5):
        ops_ref[j] = outs[j]
    v_ref[...] = outs[5]
    g_ref[...] = outs[6]
    bonus_ref[...] = outs[7]


def _proj_prep_prompt(x3, nmix, w_bf16, z0, rw, tm):
    bsz, t, _ = x3.shape
    w_ = RW_WIDTH
    row_spec = lambda width: pl.BlockSpec((bsz, tm, width), lambda i: (0, i, 0))
    last_spec = pl.BlockSpec((bsz, 1, RW_COLS), lambda i: (0, 0, 0))
    widths = (SWA_Q, SWA_KV, SWA_KV, MEM_WIDTH)
    return pl.pallas_call(
        _proj_prep_prompt_kernel,
        grid=(t // tm,),
        in_specs=[row_spec(D_MODEL), _const_spec((1, D_MODEL)), _const_spec((D_MODEL, PROJ_COLS)),
                  _const_spec((bsz, 1, RW_COLS))] + [_const_spec(p.shape) for p in rw],
        out_specs=[row_spec(w) for w in widths] + [
            last_spec,
            pl.BlockSpec((SCAN_OPS, bsz * RW_HEADS, tm // 2, LANES), lambda i: (0, 0, i, 0)),
            pl.BlockSpec((tm, 8, LANES), lambda i: (i, 0, 0)),
            row_spec(w_), row_spec(w_)],
        out_shape=[jax.ShapeDtypeStruct((bsz, t, w), F32) for w in widths] + [
            jax.ShapeDtypeStruct((bsz, 1, RW_COLS), F32),
            jax.ShapeDtypeStruct((SCAN_OPS, bsz * RW_HEADS, t // 2, LANES), F32),
            jax.ShapeDtypeStruct((t, 8, LANES), F32),
            jax.ShapeDtypeStruct((bsz, t, w_), F32), jax.ShapeDtypeStruct((bsz, t, w_), F32)],
        scratch_shapes=[pltpu.VMEM((w_ // LANES, tm, LANES), F32), pltpu.VMEM((bsz, 1, RW_COLS), F32)],
        compiler_params=_params("arbitrary"),
        name="proj_prep_prompt",
    )(x3, nmix, w_bf16, z0, *rw)


def _rwkv_prep_sample(z, zprev, rw):
    n = z.shape[0]
    w_ = RW_WIDTH
    return pl.pallas_call(
        _rwkv_prep_sample_kernel,
        grid=(1,),
        in_specs=[_const_spec((n, RW_COLS)), _const_spec((n, RW_COLS))] + [_const_spec(p.shape) for p in rw],
        out_specs=[_whole_spec((5, n, w_)), _whole_spec((n, w_)), _whole_spec((n, w_)), _whole_spec((n, w_))],
        out_shape=[jax.ShapeDtypeStruct((5, n, w_), F32)] + [jax.ShapeDtypeStruct((n, w_), F32)] * 3,
        compiler_params=_params("arbitrary"),
        name="rwkv_prep_sample",
    )(z, zprev, *rw)


SCAN_OPS = 5
SCAN_CHUNK = 32
SCAN_ACCUMULATORS = 2
N_BH = 16


def _tree_sum(parts):
    while len(parts) > 1:
        parts = [parts[i] + parts[i + 1] for i in range(0, len(parts), 2)]
    return parts[0]


def _rwkv_scan_kernel(opp_ref, oppn_ref, vv_ref, s0_ref, y_ref, sout_ref, s_ref, xr0_ref, xr1_ref):
    step = pl.program_id(0)
    tc = vv_ref.shape[0]
    nk = RW_HEAD_DIM

    def build_pair(src_ref, dst_ref, tp):
        for op in range(SCAN_OPS):
            rows = [jnp.broadcast_to(src_ref[op, bh, pl.ds(tp, 1), :], (8, LANES)) for bh in range(N_BH)]
            m_out = jnp.concatenate(rows, axis=0).T
            dst_ref[op, 2 * tp] = m_out[:nk]
            dst_ref[op, 2 * tp + 1] = m_out[nk:]

    @pl.when(step == 0)
    def _():
        s_ref[...] = s0_ref[...]

        def first(tp, carry):
            build_pair(opp_ref, xr0_ref, tp)
            return carry

        lax.fori_loop(0, tc // 2, first, 0)

    def run(cur_ref, nxt_ref):
        def opnd(op, t, k):
            return jnp.broadcast_to(cur_ref[op, t, pl.ds(k, 1), :], (8, LANES))

        def accumulate(parts, k, term):
            j = k % SCAN_ACCUMULATORS
            parts[j] = term if parts[j] is None else parts[j] + term

        def update(t, sa, next_t):
            vv = vv_ref[t]
            yacc = [None] * SCAN_ACCUMULATORS
            nacc = [None] * SCAN_ACCUMULATORS
            for k in range(nk):
                s_new = (s_ref[k] * opnd(1, t, k) + vv * opnd(3, t, k)) + sa * opnd(2, t, k)
                s_ref[k] = s_new
                accumulate(yacc, k, s_new * opnd(4, t, k))
                if next_t is not None:
                    accumulate(nacc, k, s_new * opnd(0, next_t, k))
            y_ref[t] = _tree_sum(yacc)
            return None if next_t is None else -_tree_sum(nacc)

        def pair(tp, carry):
            build_pair(oppn_ref, nxt_ref, tp)
            t0, t1 = 2 * tp, 2 * tp + 1
            accs = [None] * SCAN_ACCUMULATORS
            for k in range(nk):
                accumulate(accs, k, s_ref[k] * opnd(0, t0, k))
            sa1 = update(t0, -_tree_sum(accs), t1)
            update(t1, sa1, None)
            return carry

        lax.fori_loop(0, tc // 2, pair, 0)

    @pl.when(step % 2 == 0)
    def _():
        run(xr0_ref, xr1_ref)

    @pl.when(step % 2 == 1)
    def _():
        run(xr1_ref, xr0_ref)

    @pl.when(step == pl.num_programs(0) - 1)
    def _():
        sout_ref[...] = s_ref[...]


def _rwkv_scan(opp, vvp, s0p):
    t = vvp.shape[0]
    tc = SCAN_CHUNK
    nsteps = t // tc
    nk = RW_HEAD_DIM
    return pl.pallas_call(
        _rwkv_scan_kernel,
        grid=(nsteps,),
        in_specs=[pl.BlockSpec((SCAN_OPS, N_BH, tc // 2, LANES), lambda i: (0, 0, i, 0)),
                  pl.BlockSpec((SCAN_OPS, N_BH, tc // 2, LANES), lambda i: (0, 0, jnp.minimum(i + 1, nsteps - 1), 0)),
                  pl.BlockSpec((tc, 8, LANES), lambda i: (i, 0, 0)),
                  pl.BlockSpec((nk, 8, LANES), lambda i: (0, 0, 0))],
        out_specs=[pl.BlockSpec((tc, 8, LANES), lambda i: (i, 0, 0)),
                   pl.BlockSpec((nk, 8, LANES), lambda i: (0, 0, 0))],
        out_shape=[jax.ShapeDtypeStruct((t, 8, LANES), F32), jax.ShapeDtypeStruct((nk, 8, LANES), F32)],
        scratch_shapes=[pltpu.VMEM((nk, 8, LANES), F32),
                        pltpu.VMEM((SCAN_OPS, tc, nk, LANES), F32),
                        pltpu.VMEM((SCAN_OPS, tc, nk, LANES), F32)],
        compiler_params=_params("arbitrary"),
        name="rwkv_scan",
    )(opp, opp, vvp, s0p)


def _unpack_scan_state(sp, bsz):
    x = sp.reshape(RW_HEAD_DIM, 8, bsz, RW_HEADS, 8)
    return jnp.transpose(x, (2, 3, 1, 4, 0)).reshape(bsz, RW_HEADS, RW_HEAD_DIM, RW_HEAD_DIM)


def _rwkv_step_kernel(s_ref, ops_ref, v_ref, so_ref, y_ref):
    kk, w, b, kh, r = (ops_ref[j] for j in range(SCAN_OPS))
    for vi in range(s_ref.shape[0]):
        s = s_ref[vi]
        sa = -jnp.sum(s * kk, axis=0, keepdims=True)
        s_new = s * w + sa * b + v_ref[pl.ds(vi, 1), :] * kh
        so_ref[vi] = s_new
        y_ref[pl.ds(vi, 1), :] = jnp.sum(s_new * r, axis=0, keepdims=True)


def _rwkv_step(state_t, ops_t, v_t):
    nh, n, _, nbatch = state_t.shape
    vec = pl.BlockSpec((n, nbatch), lambda h: (h, 0))
    mat = pl.BlockSpec((None, n, n, nbatch), lambda h: (h, 0, 0, 0))
    return pl.pallas_call(
        _rwkv_step_kernel,
        grid=(nh,),
        in_specs=[mat, pl.BlockSpec((SCAN_OPS, n, nbatch), lambda h: (0, h, 0)), vec],
        out_specs=[mat, vec],
        out_shape=[jax.ShapeDtypeStruct(state_t.shape, F32), jax.ShapeDtypeStruct((nh * n, nbatch), F32)],
        compiler_params=_params("arbitrary"),
        name="rwkv_step",
    )(state_t, ops_t, v_t)


def _norm_rope(x, gain, cosf, s1, s2, ones):
    outs = []
    for j in range(x.shape[1] // LANES):
        xj = x[:, j * LANES:(j + 1) * LANES]
        ms = _segsum(xj * xj, ones) * (1.0 / SWA_HEAD_DIM)
        xn = xj * lax.rsqrt(ms + NORM_EPS) * gain
        outs.append(xn * cosf + pltpu.roll(xn, LANES - ROPE_DIM // 2, axis=1) * s1
                    + pltpu.roll(xn, ROPE_DIM // 2, axis=1) * s2)
    return outs[0] if len(outs) == 1 else jnp.concatenate(outs, axis=1)


SWA_BLOCKS_PER_STEP = 2


def _sink_softmax_pv(s, sink, v_bf16):
    m = jnp.maximum(jnp.max(s, axis=-1, keepdims=True), sink)
    p = jnp.exp(s - m)
    den = jnp.sum(p, axis=-1, keepdims=True) + jnp.exp(sink - m)
    return _dot(p.astype(BF16), v_bf16) / den


def _swa_prompt_kernel(sink_ref, q_ref, k_ref, v_ref, cos_ref, s1_ref, s2_ref, qg_ref, kg_ref, ones_ref,
                       o_ref, kkeep_ref, kprev_ref, vprev_ref):
    i = pl.program_id(1)
    blk = WINDOW
    nsub = q_ref.shape[0] // blk
    d = SWA_HEAD_DIM
    assert (d ** -0.5) == 0.125

    @pl.when(i == 0)
    def _():
        kprev_ref[...] = jnp.zeros_like(kprev_ref)
        vprev_ref[...] = jnp.zeros_like(vprev_ref)

    cosf, s1, s2, ones = cos_ref[...], s1_ref[...], s2_ref[...], ones_ref[...]
    qh = (_norm_rope(q_ref[...], qg_ref[...], cosf, s1, s2, ones) * (d ** -0.5)).astype(BF16)
    kc = _norm_rope(k_ref[...], kg_ref[...], cosf, s1, s2, ones)
    v = v_ref[...]
    r = lax.broadcasted_iota(jnp.int32, (blk, 2 * blk), 0)
    c = lax.broadcasted_iota(jnp.int32, (blk, 2 * blk), 1)
    in_band = (c >= r) & (c <= r + WINDOW)
    kprev, vprev = kprev_ref[...], vprev_ref[...]
    for sub in range(nsub):
        rows = slice(sub * blk, (sub + 1) * blk)
        kcat = jnp.concatenate([kprev, kc[rows]], axis=0).astype(BF16)
        vcat = jnp.concatenate([vprev, v[rows]], axis=0).astype(BF16)
        valid = in_band & ((c >= blk) | (i > 0)) if sub == 0 else in_band
        scores = []
        for h in range(SWA_HEADS):
            g = h // SWA_GROUPS
            s = lax.dot_general(qh[rows, h * d:(h + 1) * d], kcat[:, g * d:(g + 1) * d], (((1,), (1,)), ((), ())),
                                preferred_element_type=F32)
            scores.append(jnp.where(valid, s, -jnp.inf))
        outs = [_sink_softmax_pv(scores[h], sink_ref[h], vcat[:, (h // SWA_GROUPS) * d:(h // SWA_GROUPS + 1) * d])
                for h in range(SWA_HEADS)]
        o_ref[rows, :] = jnp.concatenate(outs, axis=1)
        kprev, vprev = kc[rows], v[rows]
    kprev_ref[...] = kprev
    vprev_ref[...] = vprev

    @pl.when(i == pl.num_programs(1) - 1)
    def _():
        kkeep_ref[...] = kprev


def _swa_prompt(q3, k3, v3, tables, qg, kg, sinks, ones):
    bsz, t, _ = q3.shape
    blk = WINDOW
    rows = SWA_BLOCKS_PER_STEP * blk
    row = lambda width: pl.BlockSpec((None, rows, width), lambda b, i: (b, i, 0))
    tab = pl.BlockSpec((rows, LANES), lambda b, i: (i, 0))
    return pl.pallas_call(
        _swa_prompt_kernel,
        grid=(bsz, t // rows),
        in_specs=[pl.BlockSpec(memory_space=pltpu.SMEM), row(SWA_Q), row(SWA_KV), row(SWA_KV), tab, tab, tab,
                  _const_spec((1, LANES)), _const_spec((1, LANES)), _const_spec((LANES, LANES))],
        out_specs=[row(SWA_Q), pl.BlockSpec((None, blk, SWA_KV), lambda b, i: (b, 0, 0))],
        out_shape=[jax.ShapeDtypeStruct((bsz, t, SWA_Q), F32), jax.ShapeDtypeStruct((bsz, blk, SWA_KV), F32)],
        scratch_shapes=[pltpu.VMEM((blk, SWA_KV), F32), pltpu.VMEM((blk, SWA_KV), F32)],
        compiler_params=_params("arbitrary", "arbitrary"),
        name="swa_prompt",
    )(sinks, q3, k3, v3, *tables, qg, kg, ones)


def _qk_rope_kernel(q_ref, k_ref, v_ref, cos_ref, s1_ref, s2_ref, qg_ref, kg_ref, ones_ref,
                    q8_ref, ko_ref, kt_ref, vt_ref):
    cosf, s1, s2, ones = cos_ref[...], s1_ref[...], s2_ref[...], ones_ref[...]
    qh = _norm_rope(q_ref[...], qg_ref[...], cosf, s1, s2, ones)
    lane_group = lax.broadcasted_iota(jnp.int32, (qh.shape[0], LANES), 1) // SWA_HEAD_DIM
    for h in range(SWA_HEADS):
        m, parity = divmod(h, 2)
        g = h // SWA_GROUPS
        piece = qh[:, m * LANES:(m + 1) * LANES]
        if parity != g:
            piece = pltpu.roll(piece, SWA_HEAD_DIM, axis=1)
        q8_ref[:, h, :] = jnp.where(lane_group == g, piece, 0.0)
    kh = _norm_rope(k_ref[...], kg_ref[...], cosf, s1, s2, ones)
    ko_ref[...] = kh
    kt_ref[...] = kh.T
    vt_ref[...] = v_ref[...].T


def _qk_rope(q, k, v, tables, qg, kg, ones):
    n = q.shape[0]
    args = (q, k, v, *tables, qg, kg, ones)
    shapes = ((n, SWA_HEADS, SWA_KV), (n, SWA_KV), (SWA_KV, n), (SWA_KV, n))
    return pl.pallas_call(
        _qk_rope_kernel,
        grid=(1,),
        in_specs=[_const_spec(a.shape) for a in args],
        out_specs=[_whole_spec(s) for s in shapes],
        out_shape=[jax.ShapeDtypeStruct(s, F32) for s in shapes],
        compiler_params=_params("arbitrary"),
        name="qk_rope_sample",
    )(*args)


def _swa_sample_kernel(sink_ref, q8_ref, kn_ref, vn_ref, knt_ref, vnt_ref, ck_ref, cv_ref, o_ref, cko_ref, cvo_ref):
    d = SWA_HEAD_DIM
    scale = d ** -0.5
    nrows = q8_ref.shape[0]
    window = ck_ref.shape[-1]
    head = lax.broadcasted_iota(jnp.int32, (SWA_HEADS, 1), 0)
    lane_group = lax.broadcasted_iota(jnp.int32, (1, SWA_KV), 1) // d
    newest = lax.broadcasted_iota(jnp.int32, (SWA_KV, window), 1) == window - 1
    sink = jnp.zeros((SWA_HEADS, 1), F32)
    for h in range(SWA_HEADS):
        sink = jnp.where(head == h, sink_ref[h], sink)
    nt = (((1,), (1,)), ((), ()))
    q8 = [q8_ref[b] for b in range(nrows)]
    kn = [kn_ref[pl.ds(b, 1), :] for b in range(nrows)]
    vn = [vn_ref[pl.ds(b, 1), :] for b in range(nrows)]
    kt = [ck_ref[b].reshape(SWA_KV, window) for b in range(nrows)]
    vt = [cv_ref[b].reshape(SWA_KV, window) for b in range(nrows)]
    s_past = [_dot(q8[b].astype(BF16), kt[b].astype(BF16)) * scale for b in range(nrows)]
    s_new = [jnp.sum(q8[b] * kn[b], axis=-1, keepdims=True) * scale for b in range(nrows)]
    probs = []
    for b in range(nrows):
        mx = jnp.maximum(jnp.maximum(jnp.max(s_past[b], axis=-1, keepdims=True), s_new[b]), sink)
        p_past, p_new = jnp.exp(s_past[b] - mx), jnp.exp(s_new[b] - mx)
        den = jnp.sum(p_past, axis=-1, keepdims=True) + p_new + jnp.exp(sink - mx)
        probs.append((p_past, p_new, den))
    for b in range(nrows):
        p_past, p_new, den = probs[b]
        o = (lax.dot_general(p_past.astype(BF16), vt[b].astype(BF16), nt, preferred_element_type=F32)
             + p_new * vn[b]) / den
        pairs = []
        for m in range(SWA_HEADS // 2):
            g = (2 * m) // SWA_GROUPS
            first, second = o[2 * m:2 * m + 1, :], o[2 * m + 1:2 * m + 2, :]
            if g == 0:
                second = pltpu.roll(second, d, axis=1)
            else:
                first = pltpu.roll(first, d, axis=1)
            pairs.append(jnp.where(lane_group == 0, first, second))
        o_ref[pl.ds(b, 1), :] = jnp.concatenate(pairs, axis=1)
        for src, col_ref, dst_ref in ((kt[b], knt_ref, cko_ref), (vt[b], vnt_ref, cvo_ref)):
            slid = jnp.where(newest, col_ref[:, b:b + 1], pltpu.roll(src, window - 1, axis=1))
            dst_ref[b] = slid.reshape(SWA_KV_HEADS, d, window)


def _swa_sample(q8, kh, v, kh_t, v_t, cache_k, cache_v, sinks, tb):
    n = q8.shape[0]
    w = cache_k.shape[-1]
    cache = pl.BlockSpec((tb, SWA_KV_HEADS, SWA_HEAD_DIM, w), lambda i: (i, 0, 0, 0))
    row = lambda width: pl.BlockSpec((tb, width), lambda i: (i, 0))
    col = pl.BlockSpec((None, SWA_KV, tb), lambda i: (i, 0, 0))
    return pl.pallas_call(
        _swa_sample_kernel,
        grid=(n // tb,),
        in_specs=[pl.BlockSpec(memory_space=pltpu.SMEM),
                  pl.BlockSpec((tb, SWA_HEADS, SWA_KV), lambda i: (i, 0, 0)), row(SWA_KV), row(SWA_KV), col, col,
                  cache, cache],
        out_specs=[row(SWA_Q), cache, cache],
        out_shape=[jax.ShapeDtypeStruct((n, SWA_Q), F32),
                   jax.ShapeDtypeStruct(cache_k.shape, F32), jax.ShapeDtypeStruct(cache_v.shape, F32)],
        compiler_params=_params("arbitrary"),
        name="swa_sample",
    )(sinks, q8, kh, v, kh_t, v_t, cache_k, cache_v)


def _mem_kv_kernel(mem_ref, g_ref, w_ref, kg_ref, mk_ref, mv_ref):
    kv = _dot(_rms(mem_ref[...], g_ref[...]).astype(BF16), w_ref[...])
    kg = kg_ref[...]
    mk_ref[...] = jnp.concatenate(
        [_rms(kv[:, h * MEM_HEAD_DIM:(h + 1) * MEM_HEAD_DIM], kg) for h in range(MEM_HEADS)], axis=1)
    mv_ref[...] = kv[:, MEM_WIDTH:]


def _mem_kv(mem, gain, w_bf16, kgain):
    bsz, m, _ = mem.shape
    out = pl.BlockSpec((None, m, MEM_WIDTH), lambda b: (b, 0, 0))
    return pl.pallas_call(
        _mem_kv_kernel,
        grid=(bsz,),
        in_specs=[pl.BlockSpec((None, m, D_MODEL), lambda b: (b, 0, 0)), _const_spec((1, D_MODEL)),
                  _const_spec((D_MODEL, 2 * MEM_WIDTH)), _const_spec((1, MEM_HEAD_DIM))],
        out_specs=[out, out],
        out_shape=[jax.ShapeDtypeStruct((bsz, m, MEM_WIDTH), F32)] * 2,
        compiler_params=_params("arbitrary"),
        name="mem_kv",
    )(mem, gain, w_bf16, kgain)


def _mem_attn_prompt_kernel(xq_ref, mk_ref, mv_ref, qg_ref, o_ref):
    xq = xq_ref[...]
    qg = qg_ref[...]
    d = MEM_HEAD_DIM
    scores = []
    for h in range(MEM_HEADS):
        lanes = slice(h * d, (h + 1) * d)
        q = _rms(xq[:, lanes], qg).astype(BF16)
        scores.append(lax.dot_general(q, mk_ref[:, lanes].astype(BF16), (((1,), (1,)), ((), ())),
                                      preferred_element_type=F32) * (d ** -0.5))
    outs = []
    for h in range(MEM_HEADS):
        s = scores[h]
        p = jnp.exp(s - jnp.max(s, axis=-1, keepdims=True))
        den = jnp.sum(p, axis=-1, keepdims=True)
        outs.append(_dot(p.astype(BF16), mv_ref[:, h * d:(h + 1) * d].astype(BF16)) / den)
    o_ref[...] = jnp.concatenate(outs, axis=1)


def _mem_attn_prompt(xq3, mk, mv, qgain, tq):
    bsz, t, _ = xq3.shape
    m = mk.shape[1]
    mem = pl.BlockSpec((None, m, MEM_WIDTH), lambda b, i: (b, 0, 0))
    row = pl.BlockSpec((None, tq, MEM_WIDTH), lambda b, i: (b, i, 0))
    return pl.pallas_call(
        _mem_attn_prompt_kernel,
        grid=(bsz, t // tq),
        in_specs=[row, mem, mem, _const_spec((1, MEM_HEAD_DIM))],
        out_specs=row,
        out_shape=jax.ShapeDtypeStruct((bsz, t, MEM_WIDTH), F32),
        compiler_params=_params("arbitrary", "arbitrary"),
        name="mem_attn_prompt",
    )(xq3, mk, mv, qgain)


def _mem_attn_sample_kernel(xq_ref, mk_ref, mv_ref, qg_ref, o_ref):
    d = MEM_HEAD_DIM
    qg = qg_ref[...]
    for b in range(xq_ref.shape[0]):
        q = _rms(xq_ref[b], qg)
        s = jnp.sum(mk_ref[b] * q, axis=-1, keepdims=True) * (d ** -0.5)
        p = jnp.exp(s - jnp.max(s, axis=0, keepdims=True))
        o_ref[b] = jnp.sum(p * mv_ref[b], axis=0) / jnp.sum(p, axis=0)


def _mem_attn_sample(xq3, mk5, mv5, qgain, tb):
    _, n, m, _, _ = mk5.shape
    mem = pl.BlockSpec((None, tb, m, MEM_HEADS, MEM_HEAD_DIM), lambda i: (0, i, 0, 0, 0))
    row = pl.BlockSpec((tb, MEM_HEADS, MEM_HEAD_DIM), lambda i: (i, 0, 0))
    return pl.pallas_call(
        _mem_attn_sample_kernel,
        grid=(n // tb,),
        in_specs=[row, mem, mem, _const_spec((1, MEM_HEAD_DIM))],
        out_specs=row,
        out_shape=jax.ShapeDtypeStruct((n, MEM_HEADS, MEM_HEAD_DIM), F32),
        compiler_params=_params("arbitrary"),
        name="mem_attn_sample",
    )(xq3, mk5, mv5, qgain)


def _values_from_tiles(y_ref):
    tm = y_ref.shape[0]
    left = lax.broadcasted_iota(jnp.int32, (tm, LANES), 1) < RW_HEAD_DIM
    groups = [[], []]
    for m in range(RW_WIDTH // LANES):
        even, odd = y_ref[:, 2 * m, :], y_ref[:, 2 * m + 1, :]
        groups[0].append(jnp.where(left, even, pltpu.roll(odd, RW_HEAD_DIM, axis=1)))
        groups[1].append(jnp.where(left, pltpu.roll(even, RW_HEAD_DIM, axis=1), odd))
    return jnp.concatenate([jnp.concatenate(g, axis=1) for g in groups], axis=0)


def _merge_ffn_kernel(x_ref, y_ref, bonus_ref, grw_ref, ob_ref, oc_ref, nmix_ref, wg_ref, lnw_ref, lnb_ref,
                      ones_ref, wa_ref, wb_ref, wc_ref, wo_ref, nffn_ref, wu_ref, wd_ref, out_ref, *, y_tiles):
    nbatch, tm, _ = x_ref.shape
    rows = lambda ref: ref[...].reshape(nbatch * tm, ref.shape[-1])
    x = rows(x_ref)
    gates = jax.nn.sigmoid(_dot(_rms(x, nmix_ref[...]).astype(BF16), wg_ref[...]))
    y = _values_from_tiles(y_ref) if y_tiles else rows(y_ref)
    ones = ones_ref[...]
    inv_n = 1.0 / RW_HEAD_DIM
    yc = y - _segsum(y, ones) * inv_n
    var = _segsum(yc * yc, ones) * inv_n
    out_a = (yc * lax.rsqrt(var + GN_EPS) * lnw_ref[...] + lnb_ref[...] + rows(bonus_ref)) * rows(grw_ref)
    merged = (gates[:, :D_MODEL] * _dot(out_a.astype(BF16), wa_ref[...])
              + gates[:, D_MODEL:2 * D_MODEL] * _dot(rows(ob_ref).astype(BF16), wb_ref[...])
              + gates[:, 2 * D_MODEL:] * _dot(rows(oc_ref).astype(BF16), wc_ref[...]))
    x1 = x + _dot(merged.astype(BF16), wo_ref[...])
    up = _dot(_rms(x1, nffn_ref[...]).astype(BF16), wu_ref[...])
    act = jnp.square(jnp.maximum(up, 0.0))
    out_ref[...] = (x1 + _dot(act.astype(BF16), wd_ref[...])).reshape(nbatch, tm, D_MODEL)


def _merge_ffn(x, y, bonus, grw, ob, oc, weights, tm):
    nbatch, t, _ = x.shape
    y_tiles = y.ndim == 3 and y.shape[1:] == (8, LANES) and nbatch == 2
    row = lambda width: pl.BlockSpec((nbatch, tm, width), lambda i: (0, i, 0))
    y_spec = pl.BlockSpec((tm, 8, LANES), lambda i: (i, 0, 0)) if y_tiles else row(RW_WIDTH)
    return pl.pallas_call(
        functools.partial(_merge_ffn_kernel, y_tiles=y_tiles),
        grid=(t // tm,),
        in_specs=[row(D_MODEL), y_spec] + [row(RW_WIDTH)] * 4 + [_const_spec(w.shape) for w in weights],
        out_specs=row(D_MODEL),
        out_shape=jax.ShapeDtypeStruct((nbatch, t, D_MODEL), F32),
        compiler_params=_params("arbitrary"),
        name="merge_ffn",
    )(x, y, bonus, grw, ob, oc, *weights)


def _rope_tables(pos):
    half = ROPE_DIM // 2
    inv_freq = jnp.power(jnp.float32(ROPE_THETA), -jnp.arange(half, dtype=F32) * (2.0 / ROPE_DIM))
    ang = pos.astype(F32)[:, None] * inv_freq[None, :]
    cos, sin = jnp.cos(ang), jnp.sin(ang)
    n = pos.shape[0]
    rest = SWA_HEAD_DIM - ROPE_DIM
    z8, zr = jnp.zeros((n, half), F32), jnp.zeros((n, rest), F32)
    cosf = jnp.concatenate([cos, cos, jnp.ones((n, rest), F32)], axis=1)
    s1 = jnp.concatenate([-sin, z8, zr], axis=1)
    s2 = jnp.concatenate([z8, sin, zr], axis=1)
    return tuple(jnp.tile(tbl, (1, LANES // SWA_HEAD_DIM)) for tbl in (cosf, s1, s2))


def kernel(x_prompt, x_sample, state_rwkv, state_rwkv_shift, cache_swa_k, cache_swa_v, cache_mem_k, cache_mem_v, mem_prompt, norm_mix, w_in, rw_mu, rw_w0, rw_w2, rw_a0, rw_a2, rw_g2, rw_k_k, rw_k_a, rw_r_k, rw_ln_w, rw_ln_b, q_norm, k_norm, swa_sinks, mem_norm, w_mem_kv, xq_norm, xk_norm, w_br_a, w_br_b, w_br_c, w_out, norm_ffn, w_up, w_down):
    bsz, t, _ = x_prompt.shape
    nb = x_sample.shape[0]
    assert w_in.shape[0] == 1 and x_sample.shape[1] == 1

    row = lambda p: p.reshape(1, -1)
    w_ = RW_WIDTH
    vcols = slice(2 * w_, 3 * w_)

    def tile_order(a, axis=-1):
        a = jnp.moveaxis(a, axis, -1)
        a = a.reshape(a.shape[:-1] + (RW_HEADS, 8, 8)).swapaxes(-3, -2).reshape(a.shape)
        return jnp.moveaxis(a, -1, axis)

    def tile_order_vcols(a):
        return jnp.concatenate([a[..., :2 * w_], tile_order(a[..., vcols]), a[..., 3 * w_:]], axis=-1)

    w_proj = tile_order_vcols(w_in[0][:, :PROJ_COLS]).astype(BF16)
    w_gate = w_in[0][:, PROJ_COLS:].astype(BF16)
    zeros_lora = jnp.zeros((64, w_), F32)
    w2pad = jnp.concatenate([rw_w2[0], zeros_lora], axis=0)
    a2pad = jnp.concatenate([zeros_lora, rw_a2[0]], axis=0)
    hi_lo = lambda w: (w.astype(BF16), (w - w.astype(BF16).astype(F32)).astype(BF16))
    head_nat = jnp.arange(w_) // RW_HEAD_DIM
    head_tile = (jnp.arange(w_) // 8) % RW_HEADS
    ones_kk = (head_nat[:, None] == head_nat[None, :]).astype(BF16)
    ones_kv = (head_nat[:, None] == head_tile[None, :]).astype(BF16)
    ones_vv = (head_tile[:, None] == head_tile[None, :]).astype(BF16)
    rw = (row(tile_order_vcols(rw_mu[0])), row(rw_w0[0]), *hi_lo(w2pad), row(rw_a0[0]), *hi_lo(a2pad),
          *hi_lo(tile_order(rw_g2[0])), row(rw_k_k[0]), row(rw_k_a[0]), row(rw_r_k[0]), ones_kk, ones_kv)
    qg = jnp.tile(row(q_norm[0]), (1, LANES // SWA_HEAD_DIM))
    kg = jnp.tile(row(k_norm[0]), (1, LANES // SWA_HEAD_DIM))
    ones_swa = _block_ones(LANES, SWA_HEAD_DIM)
    sinks = swa_sinks[0]
    merge_w = (row(norm_mix[0]), w_gate, row(tile_order(rw_ln_w[0])), row(tile_order(rw_ln_b[0])), ones_vv,
               tile_order(w_br_a[0], axis=0).astype(BF16), w_br_b[0].astype(BF16), w_br_c[0].astype(BF16),
               w_out[0].astype(BF16), row(norm_ffn[0]), w_up[0].astype(BF16), w_down[0].astype(BF16))

    q, k3, v3, xq, z_last, ops, vv, grw, bonus = _proj_prep_prompt(
        x_prompt, row(norm_mix[0]), w_proj, jnp.zeros((bsz, 1, RW_COLS), F32), rw, 256)
    yp, sp = _rwkv_scan(ops, vv, jnp.zeros((RW_HEAD_DIM, 8, LANES), F32))
    state_p = _unpack_scan_state(sp, bsz)

    tables_p = _rope_tables(jnp.arange(t, dtype=jnp.int32))
    out_b, k_keep = _swa_prompt(q, k3, v3, tables_p, qg, kg, sinks, ones_swa)
    v_keep = v3[:, t - WINDOW:]

    mk, mv = _mem_kv(mem_prompt, row(mem_norm[0]), w_mem_kv[0].astype(BF16), row(xk_norm[0]))
    out_c = _mem_attn_prompt(xq, mk, mv, row(xq_norm[0]), 512)

    y_prompt = _merge_ffn(x_prompt, yp, bonus, grw, out_b, out_c, merge_w, 128)

    xs = x_sample.reshape(nb, D_MODEL)
    zs, qs, ks, vs, xqs = _proj(xs, row(norm_mix[0]), w_proj, nb)
    ops_s, vrw_s, grw_s, bonus_s = _rwkv_prep_sample(zs, tile_order_vcols(state_rwkv_shift[0]), rw)
    state_t, y_t = _rwkv_step(jnp.transpose(state_rwkv[0], (1, 2, 3, 0)), jnp.swapaxes(ops_s, 1, 2),
                              tile_order(vrw_s).T)
    state_s = jnp.transpose(state_t, (3, 0, 1, 2))
    y_s = y_t.T

    past = cache_swa_k.shape[2]
    assert past <= WINDOW and past <= PAST_LEN
    tables_s = _rope_tables(jnp.full((1,), PAST_LEN, dtype=jnp.int32))
    q8_s, kh_s, kh_t, vs_t = _qk_rope(qs, ks, vs, tables_s, qg, kg, ones_swa)
    tb = 8
    cols = lambda a: jnp.transpose(a.reshape(SWA_KV, nb // tb, tb), (1, 0, 2))
    window_minor = lambda c: jnp.transpose(c[0], (0, 2, 3, 1))
    ob_s, ck_new, cv_new = _swa_sample(q8_s, kh_s, vs, cols(kh_t), cols(vs_t), window_minor(cache_swa_k),
                                       window_minor(cache_swa_v), sinks, tb)
    ck_new, cv_new = (jnp.transpose(c, (0, 3, 1, 2)) for c in (ck_new, cv_new))
    oc_s = _mem_attn_sample(xqs.reshape(nb, MEM_HEADS, MEM_HEAD_DIM), cache_mem_k, cache_mem_v,
                            row(xq_norm[0]), 8)
    y_sample = _merge_ffn(xs[None], tile_order(y_s.reshape(nb, w_))[None], bonus_s[None], grw_s[None],
                          ob_s.reshape(1, nb, SWA_Q), oc_s.reshape(1, nb, MEM_WIDTH), merge_w, nb)

    kv5 = lambda a, n_: a.reshape(1, n_, -1, SWA_KV_HEADS, SWA_HEAD_DIM)
    mem5 = lambda a: a.reshape(1, bsz, -1, MEM_HEADS, MEM_HEAD_DIM)
    return (y_prompt,
            y_sample.reshape(nb, 1, D_MODEL),
            state_p[None],
            tile_order_vcols(z_last[:, 0])[None],
            kv5(k_keep, bsz), kv5(v_keep, bsz),
            mem5(mk), mem5(mv),
            state_s.reshape(1, nb, RW_HEADS, RW_HEAD_DIM, RW_HEAD_DIM),
            tile_order_vcols(zs)[None],
            kv5(ck_new, nb), kv5(cv_new, nb))
```
